```python
import jax, jax.numpy as jnp
from jax import lax
import numpy as np

D_MODEL = 1024
BATCH = 16
SEQ = 2048
DEPTH = 2
DEC_BATCH = 128
DEC_SEQ = 4
PAST_LEN = 16384
PAGE_SIZE = 128

N_META = 16
N_MIXERS = 2
N_ATTN_LAYERS = (DEPTH + 1) // 2
N_RWKV_LAYERS = DEPTH // 2
RMS_EPS = 1e-6
MLA_HEADS = 8
Q_LORA = 512
KV_LORA = 256
NOPE_DIM = 128
ROPE_DIM = 64
V_DIM = 128
ROPE_THETA = 10000.0
SOFTMAX_SCALE = (NOPE_DIM + ROPE_DIM) ** -0.5
Q_BLOCK = 128
RWKV_HEAD = 64
RWKV_HEADS = D_MODEL // RWKV_HEAD
DECAY_LORA = 64
AAA_LORA = 64
GATE_LORA = 160
LNX_EPS = 64e-5
D_FF = -(-8 * D_MODEL // (3 * 256)) * 256

kernel_name = "mla_rwkv7_hybrid_decode_step"


def rmsnorm(x, g):
    xf = x.astype(jnp.float32)
    y = xf * lax.rsqrt(jnp.mean(xf * xf, axis=-1, keepdims=True) + RMS_EPS)
    return (y * g.astype(jnp.float32)).astype(x.dtype)


def rope(x, pos):
    half = x.shape[-1] // 2
    inv = ROPE_THETA ** (-jnp.arange(half, dtype=jnp.float32) / half)
    ang = pos.astype(jnp.float32)[:, None] * inv[None, :]
    ang = ang.reshape(ang.shape[:1] + (1,) * (x.ndim - 3) + (half,))
    cos, sin = jnp.cos(ang), jnp.sin(ang)
    xf = x.astype(jnp.float32)
    x1, x2 = xf[..., :half], xf[..., half:]
    return jnp.concatenate([x1 * cos - x2 * sin, x2 * cos + x1 * sin], axis=-1).astype(x.dtype)


def swiglu(h, w_gate, w_up, w_down):
    return (jax.nn.silu(h @ w_gate) * (h @ w_up)) @ w_down


def mla_project(h, pos, w_in, q_norm, kv_norm, w_qb, w_uk):
    B, T, _ = h.shape
    z = h @ w_in
    q_a, c, kr = jnp.split(z, [Q_LORA, Q_LORA + KV_LORA], axis=-1)
    q = (rmsnorm(q_a, q_norm) @ w_qb).reshape(B, T, MLA_HEADS, NOPE_DIM + ROPE_DIM)
    q_nope, q_rope = q[..., :NOPE_DIM], q[..., NOPE_DIM:]
    q_lat = jnp.einsum('bthn,rhn->bhtr', q_nope, w_uk)
    q_rope = rope(q_rope, pos).transpose(0, 2, 1, 3)
    c = rmsnorm(c, kv_norm)
    kr = rope(kr, pos)
    return q_lat, q_rope, c, kr


def mla_core(q_lat, q_rope, c, kr, q_pos, k_pos):
    s = jnp.einsum('bhtr,bsr->bhts', q_lat, c) + jnp.einsum('bhtp,bsp->bhts', q_rope, kr)
    s = s.astype(jnp.float32) * SOFTMAX_SCALE
    s = jnp.where(k_pos[None, :] <= q_pos[:, None], s, jnp.finfo(jnp.float32).min)
    p = jax.nn.softmax(s, axis=-1).astype(c.dtype)
    return jnp.einsum('bhts,bsr->bthr', p, c)


def mla_out(o_lat, w_uv, w_o):
    B, T = o_lat.shape[:2]
    o = jnp.einsum('bthr,rhv->bthv', o_lat, w_uv).reshape(B, T, MLA_HEADS * V_DIM)
    return o @ w_o


def mla_prompt_attend(q_lat, q_rope, c, kr):
    B, H, T, R = q_lat.shape
    nb = -(-T // Q_BLOCK)
    pad = nb * Q_BLOCK - T
    ql = jnp.pad(q_lat, ((0, 0), (0, 0), (0, pad), (0, 0))).reshape(B, H, nb, Q_BLOCK, R).transpose(2, 0, 1, 3, 4)
    qr = jnp.pad(q_rope, ((0, 0), (0, 0), (0, pad), (0, 0))).reshape(B, H, nb, Q_BLOCK, ROPE_DIM).transpose(2, 0, 1, 3, 4)
    k_pos = jnp.arange(T)

    def block(args):
        qlb, qrb, start = args
        return mla_core(qlb, qrb, c, kr, start + jnp.arange(Q_BLOCK), k_pos)

    o = lax.map(block, (ql, qr, jnp.arange(nb) * Q_BLOCK))
    return o.transpose(1, 0, 2, 3, 4).reshape(B, nb * Q_BLOCK, H, R)[:, :T]


def rwkv_time_mix(h, shift_prev, s0, mix, w_r, w_k, w_v, w_o, w0, w1, w2, a0, a1, a2, g1, g2, k_k, k_a, r_k, lnx_w, lnx_b):
    B, T, D = h.shape
    f32 = jnp.float32
    x_prev = jnp.concatenate([shift_prev[:, None].astype(h.dtype), h[:, :-1]], axis=1)
    xx = x_prev - h
    xr, xw, xk, xv, xa, xg = h[None] + xx[None] * mix[:, None, None, :].astype(h.dtype)
    r = (xr @ w_r).astype(f32)
    k = (xk @ w_k).astype(f32)
    v = (xv @ w_v).astype(f32)
    logw = -jax.nn.softplus(-(w0 + jnp.tanh(xw @ w1) @ w2).astype(f32)) - 0.5
    decay = jnp.exp(-jnp.exp(logw))
    a = jax.nn.sigmoid((a0 + (xa @ a1) @ a2).astype(f32))
    g = jax.nn.sigmoid(xg @ g1) @ g2
    kk = (k * k_k.astype(f32)).reshape(B, T, RWKV_HEADS, RWKV_HEAD)
    kk = kk / jnp.maximum(jnp.sqrt(jnp.sum(kk * kk, axis=-1, keepdims=True)), 1e-12)
    k = k * (1.0 + (a - 1.0) * k_a.astype(f32))
    hs = lambda t: t.reshape(B, T, RWKV_HEADS, RWKV_HEAD)
    r, k, v, a, decay = hs(r), hs(k), hs(v), hs(a), hs(decay)

    def step(S, inp):
        r_t, w_t, k_t, v_t, kk_t, a_t = inp
        sa = jnp.einsum('bhij,bhj->bhi', S, -kk_t)
        S = S * w_t[:, :, None, :] + sa[..., None] * (kk_t * a_t)[:, :, None, :] + v_t[..., None] * k_t[:, :, None, :]
        return S, jnp.einsum('bhij,bhj->bhi', S, r_t)

    seq = tuple(t.transpose(1, 0, 2, 3) for t in (r, decay, k, v, kk, a))
    s_T, y = lax.scan(step, s0.astype(f32), seq)
    y = y.transpose(1, 0, 2, 3)
    mu = jnp.mean(y, axis=-1, keepdims=True)
    var = jnp.mean(jnp.square(y - mu), axis=-1, keepdims=True)
    y = ((y - mu) * lax.rsqrt(var + LNX_EPS)).reshape(B, T, D) * lnx_w.astype(f32) + lnx_b.astype(f32)
    bonus = jnp.sum(r * k * r_k.astype(f32), axis=-1, keepdims=True) * v
    y = y + bonus.reshape(B, T, D)
    out = (y * g.astype(f32)).astype(h.dtype) @ w_o
    return out, h[:, -1], s_T.astype(s0.dtype)


def setup_inputs(seed: int = 0) -> dict:
    key = jax.random.key(seed)
    ks = iter(jax.random.split(key, 48))
    f32 = jnp.float32
    nrm = lambda shape, scale: jax.random.normal(next(ks), shape, f32) * scale
    gain = lambda shape: 1.0 + 0.02 * jax.random.normal(next(ks), shape, f32)
    NA, NR, D = N_ATTN_LAYERS, N_RWKV_LAYERS, D_MODEL
    n_pages = PAST_LEN // PAGE_SIZE
    n_used = DEC_BATCH * n_pages
    n_pool = (5 * n_used) // 4
    page_table = jax.random.permutation(next(ks), n_pool)[:n_used].reshape(DEC_BATCH, n_pages).astype(jnp.int32)
    return {
        "x_prompt": nrm((BATCH, SEQ, D), 1.0),
        "x_sample": nrm((DEC_BATCH, DEC_SEQ, D), 1.0),
        "cache_latent": nrm((NA, n_pool, PAGE_SIZE, KV_LORA), 1.0),
        "cache_krope": nrm((NA, n_pool, PAGE_SIZE, ROPE_DIM), 1.0),
        "state_shift": nrm((NR, DEC_BATCH, D), 1.0),
        "state_wkv": nrm((NR, DEC_BATCH, RWKV_HEADS, RWKV_HEAD, RWKV_HEAD), 0.1),
        "page_table": page_table,
        "meta_tokens": nrm((N_META, D), 1.0),
        "ln_mix": gain((DEPTH, D)),
        "ln_ffn": gain((DEPTH, D)),
        "ln_final": gain((D,)),
        "attn_w_in": nrm((NA, D, Q_LORA + KV_LORA + ROPE_DIM), D ** -0.5),
        "attn_q_norm": gain((NA, Q_LORA)),
        "attn_kv_norm": gain((NA, KV_LORA)),
        "attn_w_qb": nrm((NA, Q_LORA, MLA_HEADS * (NOPE_DIM + ROPE_DIM)), Q_LORA ** -0.5),
        "attn_w_uk": nrm((NA, KV_LORA, MLA_HEADS, NOPE_DIM), KV_LORA ** -0.5),
        "attn_w_uv": nrm((NA, KV_LORA, MLA_HEADS, V_DIM), KV_LORA ** -0.5),
        "attn_w_o": nrm((NA, MLA_HEADS * V_DIM, D), (MLA_HEADS * V_DIM) ** -0.5),
        "rwkv_mix": jax.random.uniform(next(ks), (NR, 6, D), f32),
        "rwkv_w_r": nrm((NR, D, D), D ** -0.5),
        "rwkv_w_k": nrm((NR, D, D), D ** -0.5),
        "rwkv_w_v": nrm((NR, D, D), D ** -0.5),
        "rwkv_w_o": nrm((NR, D, D), D ** -0.5),
        "rwkv_w0": jax.random.uniform(next(ks), (NR, D), f32, -6.0, 0.0),
        "rwkv_w1": nrm((NR, D, DECAY_LORA), D ** -0.5),
        "rwkv_w2": nrm((NR, DECAY_LORA, D), 0.1 * DECAY_LORA ** -0.5),
        "rwkv_a0": nrm((NR, D), 0.1),
        "rwkv_a1": nrm((NR, D, AAA_LORA), D ** -0.5),
        "rwkv_a2": nrm((NR, AAA_LORA, D), 0.5 * AAA_LORA ** -0.5),
        "rwkv_g1": nrm((NR, D, GATE_LORA), D ** -0.5),
        "rwkv_g2": nrm((NR, GATE_LORA, D), GATE_LORA ** -0.5),
        "rwkv_k_k": 0.85 + nrm((NR, D), 0.05),
        "rwkv_k_a": 1.0 + nrm((NR, D), 0.05),
        "rwkv_r_k": nrm((NR, RWKV_HEADS, RWKV_HEAD), 0.1),
        "rwkv_lnx_w": gain((NR, D)),
        "rwkv_lnx_b": nrm((NR, D), 0.01),
        "ffn_w_gate": nrm((DEPTH, D, D_FF), D ** -0.5),
        "ffn_w_up": nrm((DEPTH, D, D_FF), D ** -0.5),
        "ffn_w_down": nrm((DEPTH, D_FF, D), D_FF ** -0.5),
    }


def reference(x_prompt, x_sample, cache_latent, cache_krope, state_shift, state_wkv, page_table, meta_tokens, ln_mix, ln_ffn, ln_final, attn_w_in, attn_q_norm, attn_kv_norm, attn_w_qb, attn_w_uk, attn_w_uv, attn_w_o, rwkv_mix, rwkv_w_r, rwkv_w_k, rwkv_w_v, rwkv_w_o, rwkv_w0, rwkv_w1, rwkv_w2, rwkv_a0, rwkv_a1, rwkv_a2, rwkv_g1, rwkv_g2, rwkv_k_k, rwkv_k_a, rwkv_r_k, rwkv_lnx_w, rwkv_lnx_b, ffn_w_gate, ffn_w_up, ffn_w_down):
    B = x_prompt.shape[0]
    DB, DS = x_sample.shape[0], x_sample.shape[1]
    meta = jnp.broadcast_to(meta_tokens[None].astype(x_prompt.dtype), (B, N_META, D_MODEL))
    xp = jnp.concatenate([meta, x_prompt], axis=1)
    Tp = xp.shape[1]
    pos_p = jnp.arange(Tp)
    past_len = page_table.shape[1] * PAGE_SIZE
    pos_s = past_len + jnp.arange(DS)
    k_pos_s = jnp.arange(past_len + DS)
    xs = x_sample
    p_lat, p_kr, p_shift, p_wkv = [], [], [], []
    s_lat, s_kr, s_shift, s_wkv = [], [], [], []
    for i in range(DEPTH):
        j = i // N_MIXERS
        hp = rmsnorm(xp, ln_mix[i])
        hs = rmsnorm(xs, ln_mix[i])
        if i % N_MIXERS == 0:
            proj = (attn_w_in[j], attn_q_norm[j], attn_kv_norm[j], attn_w_qb[j], attn_w_uk[j])
            ql, qr, c, kr = mla_project(hp, pos_p, *proj)
            mp = mla_out(mla_prompt_attend(ql, qr, c, kr), attn_w_uv[j], attn_w_o[j])
            p_lat.append(c)
            p_kr.append(kr)
            ql_s, qr_s, c_s, kr_s = mla_project(hs, pos_s, *proj)
            c_past = cache_latent[j, page_table].reshape(DB, past_len, KV_LORA).astype(c_s.dtype)
            kr_past = cache_krope[j, page_table].reshape(DB, past_len, ROPE_DIM).astype(kr_s.dtype)
            c_all = jnp.concatenate([c_past, c_s], axis=1)
            kr_all = jnp.concatenate([kr_past, kr_s], axis=1)
            ms = mla_out(mla_core(ql_s, qr_s, c_all, kr_all, pos_s, k_pos_s), attn_w_uv[j], attn_w_o[j])
            s_lat.append(c_s)
            s_kr.append(kr_s)
        else:
            rw = (rwkv_mix[j], rwkv_w_r[j], rwkv_w_k[j], rwkv_w_v[j], rwkv_w_o[j], rwkv_w0[j], rwkv_w1[j], rwkv_w2[j], rwkv_a0[j], rwkv_a1[j], rwkv_a2[j], rwkv_g1[j], rwkv_g2[j], rwkv_k_k[j], rwkv_k_a[j], rwkv_r_k[j], rwkv_lnx_w[j], rwkv_lnx_b[j])
            shift0 = jnp.zeros((B, D_MODEL), xp.dtype)
            wkv0 = jnp.zeros((B, RWKV_HEADS, RWKV_HEAD, RWKV_HEAD), xp.dtype)
            mp, sh_p, st_p = rwkv_time_mix(hp, shift0, wkv0, *rw)
            ms, sh_s, st_s = rwkv_time_mix(hs, state_shift[j], state_wkv[j], *rw)
            p_shift.append(sh_p)
            p_wkv.append(st_p)
            s_shift.append(sh_s)
            s_wkv.append(st_s)
        xp = xp + mp
        xs = xs + ms
        xp = xp + swiglu(rmsnorm(xp, ln_ffn[i]), ffn_w_gate[i], ffn_w_up[i], ffn_w_down[i])
        xs = xs + swiglu(rmsnorm(xs, ln_ffn[i]), ffn_w_gate[i], ffn_w_up[i], ffn_w_down[i])
    y_prompt = rmsnorm(xp, ln_final)[:, N_META:]
    y_sample = rmsnorm(xs, ln_final)
    return (y_prompt, y_sample, jnp.stack(p_lat), jnp.stack(p_kr), jnp.stack(p_shift), jnp.stack(p_wkv), jnp.stack(s_lat), jnp.stack(s_kr), jnp.stack(s_shift), jnp.stack(s_wkv))
```

```python
import functools

import jax
import jax.numpy as jnp
from jax import lax
from jax.experimental import pallas as pl
from jax.experimental.pallas import tpu as pltpu

F32 = jnp.float32
BF16 = jnp.bfloat16

N_META = 16
RMS_EPS = 1e-6
MLA_HEADS = 8
Q_LORA = 512
KV_LORA = 256
NOPE_DIM = 128
ROPE_DIM = 64
V_DIM = 128
ROPE_THETA = 10000.0
SOFTMAX_SCALE = (NOPE_DIM + ROPE_DIM) ** -0.5
PAGE_SIZE = 128
RWKV_HEAD = 64
LNX_EPS = 64e-5
MASK_VALUE = -1e30

ROW_TILE = 256
SEQ_ALIGN = 128
ATTN_Q_TOKENS = 128
ATTN_K_BLOCK = 256
DECODE_PAGES = 8
FFN_CHUNK = 256
WKV_LANES = 128
WKV_STEPS = 16
VMEM_LIMIT = 56 * 1024 * 1024


def _params(*sem):
    return pltpu.CompilerParams(dimension_semantics=sem, vmem_limit_bytes=VMEM_LIMIT)


def _resident(shape):
    nd = len(shape)
    return pl.BlockSpec(shape, lambda *_: (0,) * nd, pipeline_mode=pl.Buffered(1))


def _rows(width, tile=ROW_TILE):
    return pl.BlockSpec((tile, width), lambda i: (i, 0))


def _rms(x, g):
    return x * lax.rsqrt(jnp.mean(x * x, axis=-1, keepdims=True) + RMS_EPS) * g


def _sigmoid(x):
    return 1.0 / (1.0 + jnp.exp(-x))


def _dot(a, b):
    return jnp.dot(a, b, preferred_element_type=F32)


def _dot_nt(a, b):
    return lax.dot_general(a, b, (((1,), (1,)), ((), ())), preferred_element_type=F32)


def _mla_in_kernel(x_ref, cos_ref, sin_ref, g_ref, win_ref, qn_ref, kvn_ref, wqb_ref, wuk_ref,
                   qlat_ref, qrope_ref, c_ref, kr_ref, cb_ref, krb_ref):
    h = _rms(x_ref[...], g_ref[...]).astype(BF16)
    z = _dot(h, win_ref[...])
    cos = cos_ref[...]
    sin = sin_ref[...]
    c = _rms(z[:, Q_LORA:Q_LORA + KV_LORA], kvn_ref[...])
    kr = z[:, 768:832] * cos[:, :ROPE_DIM] + z[:, 896:960] * sin[:, :ROPE_DIM]
    c_ref[...] = c
    kr_ref[...] = kr
    cb_ref[...] = c.astype(BF16)
    krb_ref[...] = kr.astype(BF16)
    qn = _rms(z[:, :Q_LORA], qn_ref[...]).astype(BF16)
    q = _dot(qn, wqb_ref[...])
    nope_w = MLA_HEADS * NOPE_DIM
    rope_w = MLA_HEADS * ROPE_DIM
    for p in range(rope_w // 128):
        lo = nope_w + p * 128
        qr = q[:, lo:lo + 128] * cos + q[:, lo + rope_w:lo + rope_w + 128] * sin
        qrope_ref[:, p * 128:(p + 1) * 128] = (qr * SOFTMAX_SCALE).astype(BF16)
    for hh in range(MLA_HEADS):
        ql = _dot(q[:, hh * NOPE_DIM:(hh + 1) * NOPE_DIM].astype(BF16), wuk_ref[hh])
        qlat_ref[:, hh * KV_LORA:(hh + 1) * KV_LORA] = (ql * SOFTMAX_SCALE).astype(BF16)


def _mla_in(x, cos, sin, g, w_in, q_norm, kv_norm, w_qb, w_uk):
    m = x.shape[0]
    d = x.shape[1]
    return pl.pallas_call(
        _mla_in_kernel,
        grid=(m // ROW_TILE,),
        in_specs=[_rows(d), _rows(128), _rows(128), _resident(g.shape), _resident(w_in.shape),
                  _resident(q_norm.shape), _resident(kv_norm.shape), _resident(w_qb.shape),
                  _resident(w_uk.shape)],
        out_specs=[_rows(MLA_HEADS * KV_LORA), _rows(MLA_HEADS * ROPE_DIM), _rows(KV_LORA),
                   _rows(ROPE_DIM), _rows(KV_LORA), _rows(ROPE_DIM)],
        out_shape=[jax.ShapeDtypeStruct((m, MLA_HEADS * KV_LORA), BF16),
                   jax.ShapeDtypeStruct((m, MLA_HEADS * ROPE_DIM), BF16),
                   jax.ShapeDtypeStruct((m, KV_LORA), F32),
                   jax.ShapeDtypeStruct((m, ROPE_DIM), F32),
                   jax.ShapeDtypeStruct((m, KV_LORA), BF16),
                   jax.ShapeDtypeStruct((m, ROPE_DIM), BF16)],
        compiler_params=_params("parallel"),
        name="mla_in",
    )(x, cos, sin, g, w_in, q_norm, kv_norm, w_qb, w_uk)


def _attn_prompt_kernel(ql_ref, qr_ref, c_ref, kr_ref, o_ref, m_scr, l_scr, acc_scr):
    qi = pl.program_id(1)
    rows = ATTN_Q_TOKENS * MLA_HEADS
    n_blocks = (qi * ATTN_Q_TOKENS + ATTN_Q_TOKENS - 1) // ATTN_K_BLOCK + 1
    m_scr[...] = jnp.full(m_scr.shape, MASK_VALUE, F32)
    l_scr[...] = jnp.zeros(l_scr.shape, F32)
    acc_scr[...] = jnp.zeros(acc_scr.shape, F32)

    def body(kj, carry):
        off = pl.multiple_of(kj * ATTN_K_BLOCK, ATTN_K_BLOCK)
        cb = c_ref[pl.ds(off, ATTN_K_BLOCK), :]
        krb = kr_ref[pl.ds(off, ATTN_K_BLOCK), :]
        s = _dot_nt(ql_ref[...], cb) + _dot_nt(qr_ref[...], krb)
        q_pos = qi * ATTN_Q_TOKENS + lax.broadcasted_iota(jnp.int32, (rows, 1), 0) // MLA_HEADS
        k_pos = off + lax.broadcasted_iota(jnp.int32, (1, ATTN_K_BLOCK), 1)
        s = jnp.where(k_pos <= q_pos, s, MASK_VALUE)
        m_prev = m_scr[...]
        m_new = jnp.maximum(m_prev, jnp.max(s, axis=-1, keepdims=True))
        alpha = jnp.exp(m_prev - m_new)
        p = jnp.exp(s - m_new)
        l_scr[...] = alpha * l_scr[...] + jnp.sum(p, axis=-1, keepdims=True)
        acc_scr[...] = alpha * acc_scr[...] + _dot(p.astype(BF16), cb)
        m_scr[...] = m_new
        return carry

    lax.fori_loop(0, n_blocks, body, 0)
    o_ref[...] = (acc_scr[...] / l_scr[...]).astype(BF16)


def _attn_prompt(ql, qr, cb, krb):
    b, rows_total, _ = ql.shape
    tk = cb.shape[1]
    rows = ATTN_Q_TOKENS * MLA_HEADS
    return pl.pallas_call(
        _attn_prompt_kernel,
        grid=(b, rows_total // rows),
        in_specs=[pl.BlockSpec((None, rows, KV_LORA), lambda i, j: (i, j, 0)),
                  pl.BlockSpec((None, rows, ROPE_DIM), lambda i, j: (i, j, 0)),
                  pl.BlockSpec((None, tk, KV_LORA), lambda i, j: (i, 0, 0)),
                  pl.BlockSpec((None, tk, ROPE_DIM), lambda i, j: (i, 0, 0))],
        out_specs=pl.BlockSpec((None, rows, KV_LORA), lambda i, j: (i, j, 0)),
        out_shape=jax.ShapeDtypeStruct((b, rows_total, KV_LORA), BF16),
        scratch_shapes=[pltpu.VMEM((rows, 1), F32), pltpu.VMEM((rows, 1), F32),
                        pltpu.VMEM((rows, KV_LORA), F32)],
        compiler_params=_params("parallel", "arbitrary"),
        name="attn_prompt",
    )(ql, qr, cb, krb)


def _attn_decode_kernel(pt_ref, ql_ref, qr_ref, cn_ref, krn_ref, *rest, n_new):
    pages = rest[:DECODE_PAGES]
    kpages = rest[DECODE_PAGES:2 * DECODE_PAGES]
    o_ref, m_scr, l_scr, acc_scr = rest[2 * DECODE_PAGES:]
    g = pl.program_id(1)
    ql = ql_ref[...]
    qr = qr_ref[...]
    rows = ql.shape[0]

    @pl.when(g == 0)
    def _():
        qlf = ql.astype(F32)
        qrf = qr.astype(F32)
        cn = cn_ref[...].astype(BF16).astype(F32)
        krn = krn_ref[...].astype(BF16).astype(F32)
        q_tok = lax.broadcasted_iota(jnp.int32, (rows, 1), 0) // MLA_HEADS
        cols = []
        for t in range(n_new):
            s_t = (jnp.sum(qlf * cn[t:t + 1, :], axis=-1, keepdims=True)
                   + jnp.sum(qrf * krn[t:t + 1, :], axis=-1, keepdims=True))
            cols.append(jnp.where(q_tok >= t, s_t, MASK_VALUE))
        m0 = cols[0]
        for t in range(1, n_new):
            m0 = jnp.maximum(m0, cols[t])
        l0 = jnp.zeros((rows, 1), F32)
        acc0 = jnp.zeros((rows, KV_LORA), F32)
        for t in range(n_new):
            p_t = jnp.exp(cols[t] - m0).astype(BF16).astype(F32)
            l0 = l0 + jnp.exp(cols[t] - m0)
            acc0 = acc0 + p_t * cn[t:t + 1, :]
        m_scr[...] = m0
        l_scr[...] = l0
        acc_scr[...] = acc0

    kbs = [pages[i][...].astype(BF16) for i in range(DECODE_PAGES)]
    s = jnp.concatenate(
        [_dot_nt(ql, kbs[i]) + _dot_nt(qr, kpages[i][...].astype(BF16)) for i in range(DECODE_PAGES)],
        axis=-1)
    m_prev = m_scr[...]
    m_new = jnp.maximum(m_prev, jnp.max(s, axis=-1, keepdims=True))
    alpha = jnp.exp(m_prev - m_new)
    p = jnp.exp(s - m_new)
    l_scr[...] = alpha * l_scr[...] + jnp.sum(p, axis=-1, keepdims=True)
    acc = alpha * acc_scr[...]
    pb = p.astype(BF16)
    for i in range(DECODE_PAGES):
        acc = acc + _dot(pb[:, i * PAGE_SIZE:(i + 1) * PAGE_SIZE], kbs[i])
    acc_scr[...] = acc
    m_scr[...] = m_new

    @pl.when(g == pl.num_programs(1) - 1)
    def _():
        o_ref[...] = (acc_scr[...] / l_scr[...]).astype(BF16)


def _attn_decode(page_table, ql, qr, c_new, kr_new, cache_latent, cache_krope):
    db, rows, _ = ql.shape
    n_new = c_new.shape[1]
    n_pages = page_table.shape[1]

    def page_spec(width, i):
        return pl.BlockSpec((None, None, PAGE_SIZE, width),
                            lambda b, g, pt: (0, pt[b, g * DECODE_PAGES + i], 0, 0))

    grid_spec = pltpu.PrefetchScalarGridSpec(
        num_scalar_prefetch=1,
        grid=(db, n_pages // DECODE_PAGES),
        in_specs=([pl.BlockSpec((None, rows, KV_LORA), lambda b, g, pt: (b, 0, 0)),
                   pl.BlockSpec((None, rows, ROPE_DIM), lambda b, g, pt: (b, 0, 0)),
                   pl.BlockSpec((None, n_new, KV_LORA), lambda b, g, pt: (b, 0, 0)),
                   pl.BlockSpec((None, n_new, ROPE_DIM), lambda b, g, pt: (b, 0, 0))]
                  + [page_spec(KV_LORA, i) for i in range(DECODE_PAGES)]
                  + [page_spec(ROPE_DIM, i) for i in range(DECODE_PAGES)]),
        out_specs=pl.BlockSpec((None, rows, KV_LORA), lambda b, g, pt: (b, 0, 0)),
        scratch_shapes=[pltpu.VMEM((rows, 1), F32), pltpu.VMEM((rows, 1), F32),
                        pltpu.VMEM((rows, KV_LORA), F32)],
    )
    return pl.pallas_call(
        functools.partial(_attn_decode_kernel, n_new=n_new),
        grid_spec=grid_spec,
        out_shape=jax.ShapeDtypeStruct((db, rows, KV_LORA), BF16),
        compiler_params=_params("parallel", "arbitrary"),
        name="attn_decode",
    )(page_table, ql, qr, c_new, kr_new,
      *([cache_latent] * DECODE_PAGES), *([cache_krope] * DECODE_PAGES))


def _mla_out_kernel(o_ref, x_ref, wuv_ref, wo_ref, out_ref):
    us = [_dot(o_ref[:, hh * KV_LORA:(hh + 1) * KV_LORA], wuv_ref[hh]).astype(BF16)
          for hh in range(MLA_HEADS)]
    out_ref[...] = x_ref[...] + _dot(jnp.concatenate(us, axis=-1), wo_ref[...])


def _mla_out(o, x, w_uv, w_o):
    m, d = x.shape
    return pl.pallas_call(
        _mla_out_kernel,
        grid=(m // ROW_TILE,),
        in_specs=[_rows(o.shape[1]), _rows(d), _resident(w_uv.shape), _resident(w_o.shape)],
        out_specs=_rows(d),
        out_shape=jax.ShapeDtypeStruct((m, d), F32),
        compiler_params=_params("parallel"),
        name="mla_out",
    )(o, x, w_uv, w_o)


def _ffn_kernel(x_ref, g_ref, wgu_ref, wd_ref, gn_ref, x_out_ref, hn_ref):
    x = x_ref[...]
    h = _rms(x, g_ref[...]).astype(BF16)
    acc = x
    for ci in range(wgu_ref.shape[0]):
        gu = _dot(h, wgu_ref[ci])
        gate = gu[:, :FFN_CHUNK]
        act = (gate * _sigmoid(gate) * gu[:, FFN_CHUNK:]).astype(BF16)
        acc = acc + _dot(act, wd_ref[ci])
    x_out_ref[...] = acc
    hn_ref[...] = _rms(acc, gn_ref[...])


def _ffn(x, g, wgu, wd, g_next):
    m, d = x.shape
    return pl.pallas_call(
        _ffn_kernel,
        grid=(m // ROW_TILE,),
        in_specs=[_rows(d), _resident(g.shape), _resident(wgu.shape), _resident(wd.shape),
                  _resident(g_next.shape)],
        out_specs=[_rows(d), _rows(d)],
        out_shape=[jax.ShapeDtypeStruct((m, d), F32), jax.ShapeDtypeStruct((m, d), F32)],
        compiler_params=_params("parallel"),
        name="ffn",
    )(x, g, wgu, wd, g_next)


def _time_mix_kernel(h_ref, xp_ref, mix_ref, wr_ref, wk_ref, wv_ref, w0_ref, w1_ref, w2_ref,
                     a0_ref, a1_ref, a2_ref, g1_ref, g2_ref,
                     r_ref, w_ref, k_ref, v_ref, a_ref, g_ref):
    h = h_ref[...]
    xx = xp_ref[...] - h
    mix = mix_ref[...]

    def mixed(i):
        return (h + xx * mix[i:i + 1, :]).astype(BF16)

    r_ref[...] = _dot(mixed(0), wr_ref[...])
    lw = jnp.tanh(_dot(mixed(1), w1_ref[...])).astype(BF16)
    z = -(w0_ref[...] + _dot(lw, w2_ref[...]))
    softplus = jnp.maximum(z, 0.0) + jnp.log(1.0 + jnp.exp(-jnp.abs(z)))
    w_ref[...] = jnp.exp(-jnp.exp(-softplus - 0.5))
    k_ref[...] = _dot(mixed(2), wk_ref[...])
    v_ref[...] = _dot(mixed(3), wv_ref[...])
    al = _dot(mixed(4), a1_ref[...]).astype(BF16)
    a_ref[...] = _sigmoid(a0_ref[...] + _dot(al, a2_ref[...]))
    gl = _sigmoid(_dot(mixed(5), g1_ref[...])).astype(BF16)
    g_ref[...] = _dot(gl, g2_ref[...])


def _time_mix(h, xprev, mix, w_r, w_k, w_v, w0, w1, w2, a0, a1, a2, g1, g2):
    m, d = h.shape
    weights = (mix, w_r, w_k, w_v, w0, w1, w2, a0, a1, a2, g1, g2)
    return pl.pallas_call(
        _time_mix_kernel,
        grid=(m // ROW_TILE,),
        in_specs=[_rows(d), _rows(d)] + [_resident(w.shape) for w in weights],
        out_specs=[_rows(d)] * 6,
        out_shape=[jax.ShapeDtypeStruct((m, d), F32)] * 6,
        compiler_params=_params("parallel"),
        name="time_mix",
    )(h, xprev, *weights)


def _wkv_kernel(r_ref, w_ref, k_ref, v_ref, a_ref, s0_ref, kk_ref, ka_ref, rk_ref, lnw_ref, lnb_ref,
                y_ref, st_ref, s_scr, y_scr):
    step_block = pl.program_id(1)
    n = s_scr.shape[0]

    @pl.when(step_block == 0)
    def _():
        s_scr[...] = s0_ref[...]

    k_k = kk_ref[...]
    k_a = ka_ref[...]
    r_k = rk_ref[...]
    lnw = lnw_ref[...]
    lnb = lnb_ref[...]

    def step(t, carry):
        r = r_ref[t]
        w = w_ref[t]
        k = k_ref[t]
        a = a_ref[t]
        kk = k * k_k
        kk = kk / jnp.maximum(jnp.sqrt(jnp.sum(kk * kk, axis=0, keepdims=True)), 1e-12)
        b = kk * a
        k = k * (1.0 + (a - 1.0) * k_a)

        def row(i, c2):
            s_i = s_scr[i]
            sa = -jnp.sum(s_i * kk, axis=0, keepdims=True)
            v_i = v_ref[t, pl.ds(i, 1), :]
            s_i = s_i * w + sa * b + v_i * k
            s_scr[i] = s_i
            y_scr[pl.ds(i, 1), :] = jnp.sum(s_i * r, axis=0, keepdims=True)
            return c2

        lax.fori_loop(0, n, row, 0, unroll=8)
        y = y_scr[...]
        mu = jnp.mean(y, axis=0, keepdims=True)
        yc = y - mu
        var = jnp.mean(yc * yc, axis=0, keepdims=True)
        bonus = jnp.sum(r * k * r_k, axis=0, keepdims=True) * v_ref[t]
        y_ref[t] = yc * lax.rsqrt(var + LNX_EPS) * lnw + lnb + bonus
        return carry

    lax.fori_loop(0, r_ref.shape[0], step, 0)

    @pl.when(step_block == pl.num_programs(1) - 1)
    def _():
        st_ref[...] = s_scr[...]


def _wkv(r, w, k, v, a, s0, k_k, k_a, r_k, lnw, lnb, steps_per_block):
    t, n, lanes = r.shape
    seq = pl.BlockSpec((steps_per_block, n, WKV_LANES), lambda i, j: (j, 0, i))
    state = pl.BlockSpec((n, n, WKV_LANES), lambda i, j: (0, 0, i))
    table = pl.BlockSpec((n, WKV_LANES), lambda i, j: (0, i))
    return pl.pallas_call(
        _wkv_kernel,
        grid=(lanes // WKV_LANES, t // steps_per_block),
        in_specs=[seq] * 5 + [state] + [table] * 5,
        out_specs=[seq, state],
        out_shape=[jax.ShapeDtypeStruct((t, n, lanes), F32), jax.ShapeDtypeStruct((n, n, lanes), F32)],
        scratch_shapes=[pltpu.VMEM((n, n, WKV_LANES), F32), pltpu.VMEM((n, WKV_LANES), F32)],
        compiler_params=_params("parallel", "arbitrary"),
        name="wkv",
    )(r, w, k, v, a, s0, k_k, k_a, r_k, lnw, lnb)


def _rwkv_out_kernel(y_ref, g_ref, x_ref, wo_ref, out_ref):
    out_ref[...] = x_ref[...] + _dot((y_ref[...] * g_ref[...]).astype(BF16), wo_ref[...])


def _rwkv_out(y, g, x, w_o):
    m, d = x.shape
    return pl.pallas_call(
        _rwkv_out_kernel,
        grid=(m // ROW_TILE,),
        in_specs=[_rows(d), _rows(d), _rows(d), _resident(w_o.shape)],
        out_specs=_rows(d),
        out_shape=jax.ShapeDtypeStruct((m, d), F32),
        compiler_params=_params("parallel"),
        name="rwkv_out",
    )(y, g, x, w_o)


def _rope_tables(pos, reps):
    half = ROPE_DIM // 2
    inv = ROPE_THETA ** (-jnp.arange(half, dtype=F32) / half)
    ang = pos.astype(F32)[:, None] * inv[None, :]
    cos, sin = jnp.cos(ang), jnp.sin(ang)
    cos = jnp.tile(jnp.concatenate([cos, cos], axis=-1), (reps, 128 // ROPE_DIM))
    sin = jnp.tile(jnp.concatenate([-sin, sin], axis=-1), (reps, 128 // ROPE_DIM))
    return cos, sin


def _swap_halves(w):
    half = w.shape[-1] // 2
    return jnp.concatenate([w[..., half:], w[..., :half]], axis=-1)


def _to_lanes(x, batch, steps, heads):
    x = x.reshape(batch, -1, heads, RWKV_HEAD)[:, :steps]
    return x.transpose(1, 3, 0, 2).reshape(steps, RWKV_HEAD, batch * heads)


def _from_lanes(y, batch, heads, padded_steps):
    steps = y.shape[0]
    y = y.reshape(steps, RWKV_HEAD, batch, heads).transpose(2, 0, 3, 1)
    y = jnp.pad(y, ((0, 0), (0, padded_steps - steps), (0, 0), (0, 0)))
    return y.reshape(batch * padded_steps, heads * RWKV_HEAD)


def _head_table(p, batch, heads):
    return jnp.tile(p.reshape(heads, RWKV_HEAD).T, (1, batch))


def kernel(x_prompt, x_sample, cache_latent, cache_krope, state_shift, state_wkv, page_table, meta_tokens, ln_mix, ln_ffn, ln_final, attn_w_in, attn_q_norm, attn_kv_norm, attn_w_qb, attn_w_uk, attn_w_uv, attn_w_o, rwkv_mix, rwkv_w_r, rwkv_w_k, rwkv_w_v, rwkv_w_o, rwkv_w0, rwkv_w1, rwkv_w2, rwkv_a0, rwkv_a1, rwkv_a2, rwkv_g1, rwkv_g2, rwkv_k_k, rwkv_k_a, rwkv_r_k, rwkv_lnx_w, rwkv_lnx_b, ffn_w_gate, ffn_w_up, ffn_w_down):
    nb, seq, d = x_prompt.shape
    db, ds, _ = x_sample.shape
    t_real = seq + N_META
    t_pad = -(-t_real // SEQ_ALIGN) * SEQ_ALIGN
    past_len = page_table.shape[1] * PAGE_SIZE
    heads = d // RWKV_HEAD
    row = lambda p: p.reshape(1, -1).astype(F32)

    meta = jnp.broadcast_to(meta_tokens[None].astype(x_prompt.dtype), (nb, N_META, d))
    xp = jnp.concatenate([meta, x_prompt, jnp.zeros((nb, t_pad - t_real, d), x_prompt.dtype)], axis=1)
    xp = xp.reshape(nb * t_pad, d)
    xs = x_sample.reshape(db * ds, d)
    cos_p, sin_p = _rope_tables(jnp.arange(t_pad), nb)
    cos_s, sin_s = _rope_tables(past_len + jnp.arange(ds), db)

    w_in = attn_w_in[0]
    kr_cols = w_in[:, Q_LORA + KV_LORA:]
    zeros64 = jnp.zeros((d, 64), w_in.dtype)
    w_in_ext = jnp.concatenate([w_in, zeros64, _swap_halves(kr_cols), zeros64], axis=-1).astype(BF16)
    w_qb = attn_w_qb[0].reshape(Q_LORA, MLA_HEADS, NOPE_DIM + ROPE_DIM)
    q_rope_cols = w_qb[:, :, NOPE_DIM:]
    w_qb_ext = jnp.concatenate(
        [w_qb[:, :, :NOPE_DIM].reshape(Q_LORA, -1), q_rope_cols.reshape(Q_LORA, -1),
         _swap_halves(q_rope_cols).reshape(Q_LORA, -1)], axis=-1).astype(BF16)
    w_uk_t = attn_w_uk[0].transpose(1, 2, 0).astype(BF16)
    w_uv = attn_w_uv[0].transpose(1, 0, 2).astype(BF16)
    w_o = attn_w_o[0].astype(BF16)
    mla_w = (row(ln_mix[0]), w_in_ext, row(attn_q_norm[0]), row(attn_kv_norm[0]), w_qb_ext, w_uk_t)

    qlat_p, qrope_p, c_p, kr_p, cb_p, krb_p = _mla_in(xp, cos_p, sin_p, *mla_w)
    qlat_s, qrope_s, c_s, kr_s, _, _ = _mla_in(xs, cos_s, sin_s, *mla_w)

    k_len = -(-t_pad // ATTN_K_BLOCK) * ATTN_K_BLOCK
    pad_keys = lambda k: jnp.pad(k.reshape(nb, t_pad, -1), ((0, 0), (0, k_len - t_pad), (0, 0)))
    o_p = _attn_prompt(qlat_p.reshape(nb, t_pad * MLA_HEADS, KV_LORA),
                       qrope_p.reshape(nb, t_pad * MLA_HEADS, ROPE_DIM),
                       pad_keys(cb_p), pad_keys(krb_p)).reshape(nb * t_pad, MLA_HEADS * KV_LORA)
    o_s = _attn_decode(page_table,
                       qlat_s.reshape(db, ds * MLA_HEADS, KV_LORA),
                       qrope_s.reshape(db, ds * MLA_HEADS, ROPE_DIM),
                       c_s.reshape(db, ds, KV_LORA), kr_s.reshape(db, ds, ROPE_DIM),
                       cache_latent[:1], cache_krope[:1]).reshape(db * ds, MLA_HEADS * KV_LORA)

    def ffn_weights(i):
        n_chunks = ffn_w_gate.shape[2] // FFN_CHUNK
        wg = ffn_w_gate[i].reshape(d, n_chunks, FFN_CHUNK)
        wu = ffn_w_up[i].reshape(d, n_chunks, FFN_CHUNK)
        wgu = jnp.concatenate([wg, wu], axis=-1).transpose(1, 0, 2).astype(BF16)
        wd = ffn_w_down[i].reshape(n_chunks, FFN_CHUNK, d).astype(BF16)
        return wgu, wd

    ffn0 = (row(ln_ffn[0]),) + ffn_weights(0) + (row(ln_mix[1]),)
    xp, hp = _ffn(_mla_out(o_p, xp, w_uv, w_o), *ffn0)
    xs, hs = _ffn(_mla_out(o_s, xs, w_uv, w_o), *ffn0)

    hp3 = hp.reshape(nb, t_pad, d)
    hs3 = hs.reshape(db, ds, d)
    prev_p = jnp.concatenate([jnp.zeros((nb, 1, d), F32), hp3[:, :-1]], axis=1).reshape(nb * t_pad, d)
    prev_s = jnp.concatenate([state_shift[0][:, None].astype(F32), hs3[:, :-1]], axis=1).reshape(db * ds, d)

    def lora(w_down, w_up):
        rank = w_down.shape[1]
        pad = -(-rank // 128) * 128 - rank
        return (jnp.pad(w_down, ((0, 0), (0, pad))).astype(BF16),
                jnp.pad(w_up, ((0, pad), (0, 0))).astype(BF16))

    w1, w2 = lora(rwkv_w1[0], rwkv_w2[0])
    a1, a2 = lora(rwkv_a1[0], rwkv_a2[0])
    g1, g2 = lora(rwkv_g1[0], rwkv_g2[0])
    tm_w = (rwkv_mix[0].astype(F32), rwkv_w_r[0].astype(BF16), rwkv_w_k[0].astype(BF16),
            rwkv_w_v[0].astype(BF16), row(rwkv_w0[0]), w1, w2, row(rwkv_a0[0]), a1, a2, g1, g2)
    tables = (rwkv_k_k[0], rwkv_k_a[0], rwkv_r_k[0].reshape(-1), rwkv_lnx_w[0], rwkv_lnx_b[0])
    w_o1 = rwkv_w_o[0].astype(BF16)

    def time_mixing(h, prev, x, batch, steps, padded_steps, s0, steps_per_block):
        r, w, k, v, a, g = _time_mix(h, prev, *tm_w)
        seqs = [_to_lanes(u, batch, steps, heads) for u in (r, w, k, v, a)]
        tabs = [_head_table(p.astype(F32), batch, heads) for p in tables]
        y, s_t = _wkv(*seqs, s0, *tabs, steps_per_block)
        y = _from_lanes(y, batch, heads, padded_steps)
        s_t = s_t.reshape(RWKV_HEAD, RWKV_HEAD, batch, heads).transpose(2, 3, 0, 1)
        return _rwkv_out(y, g, x, w_o1), s_t

    s0_p = jnp.zeros((RWKV_HEAD, RWKV_HEAD, nb * heads), F32)
    s0_s = state_wkv[0].astype(F32).transpose(2, 3, 0, 1).reshape(RWKV_HEAD, RWKV_HEAD, db * heads)
    xp, wkv_p = time_mixing(hp, prev_p, xp, nb, t_real, t_pad, s0_p, WKV_STEPS)
    xs, wkv_s = time_mixing(hs, prev_s, xs, db, ds, ds, s0_s, ds)

    ffn1 = (row(ln_ffn[1]),) + ffn_weights(1) + (row(ln_final),)
    _, yp = _ffn(xp, *ffn1)
    _, ys = _ffn(xs, *ffn1)

    seq3 = lambda u: u.reshape(nb, t_pad, -1)[:, :t_real]
    return (yp.reshape(nb, t_pad, d)[:, N_META:t_real],
            ys.reshape(db, ds, d),
            seq3(c_p)[None],
            seq3(kr_p)[None],
            hp3[:, t_real - 1][None],
            wkv_p.astype(state_wkv.dtype)[None],
            c_s.reshape(db, ds, KV_LORA)[None],
            kr_s.reshape(db, ds, ROPE_DIM)[None],
            hs3[:, -1][None],
            wkv_s.astype(state_wkv.dtype)[None])
```

```python
import functools

import jax
import jax.numpy as jnp
from jax import lax
from jax.experimental import pallas as pl
from jax.experimental.pallas import tpu as pltpu

F32 = jnp.float32
BF16 = jnp.bfloat16

N_META = 16
RMS_EPS = 1e-6
MLA_HEADS = 8
Q_LORA = 512
KV_LORA = 256
NOPE_DIM = 128
ROPE_DIM = 64
V_DIM = 128
ROPE_THETA = 10000.0
SOFTMAX_SCALE = (NOPE_DIM + ROPE_DIM) ** -0.5
PAGE_SIZE = 128
RWKV_HEAD = 64
LNX_EPS = 64e-5
MASK_VALUE = -1e30

LANES = 128
QK_DIM = KV_LORA + LANES
ROW_TILE = 256
SEQ_ALIGN = 128
ATTN_Q_TOKENS = 128
ATTN_ROW_CHUNK = 512
ATTN_K_GROUP = 4
DECODE_PAGES = 32
DECODE_GROUPS = 2
FFN_CHUNK = 256
WKV_LANES = 128
WKV_STEPS = 16
VMEM_LIMIT = 56 * 1024 * 1024


def _params(*sem):
    return pltpu.CompilerParams(dimension_semantics=sem, vmem_limit_bytes=VMEM_LIMIT)


def _resident(shape):
    nd = len(shape)
    return pl.BlockSpec(shape, lambda *_: (0,) * nd, pipeline_mode=pl.Buffered(1))


def _rows(width, tile=ROW_TILE):
    return pl.BlockSpec((tile, width), lambda i: (i, 0))


def _rms(x, g):
    return x * lax.rsqrt(jnp.mean(x * x, axis=-1, keepdims=True) + RMS_EPS) * g


def _sigmoid(x):
    return 1.0 / (1.0 + jnp.exp(-x))


def _dot(a, b):
    return jnp.dot(a, b, preferred_element_type=F32)


def _dot_nt(a, b):
    return lax.dot_general(a, b, (((1,), (1,)), ((), ())), preferred_element_type=F32)


def _mla_in_kernel(x_ref, cos_ref, sin_ref, g_ref, win_ref, qn_ref, kvn_ref, wqb_ref, wuk_ref,
                   q_ref, kv_ref, c_ref, kr_ref):
    h = _rms(x_ref[...], g_ref[...]).astype(BF16)
    z = _dot(h, win_ref[...])
    cos = cos_ref[...]
    sin = sin_ref[...]
    c = _rms(z[:, Q_LORA:Q_LORA + KV_LORA], kvn_ref[...])
    kr_lo = Q_LORA + KV_LORA
    kr = z[:, kr_lo:kr_lo + LANES] * cos + z[:, kr_lo + LANES:kr_lo + 2 * LANES] * sin
    c_ref[...] = c
    kr_ref[...] = kr[:, :ROPE_DIM]
    kv_ref[:, :KV_LORA] = c.astype(BF16)
    kv_ref[:, KV_LORA:] = kr.astype(BF16)
    qn = _rms(z[:, :Q_LORA], qn_ref[...]).astype(BF16)
    q = _dot(qn, wqb_ref[...])
    rope_lo = MLA_HEADS * NOPE_DIM
    swap_lo = rope_lo + MLA_HEADS * LANES
    for hh in range(MLA_HEADS):
        ql = _dot(q[:, hh * NOPE_DIM:(hh + 1) * NOPE_DIM].astype(BF16), wuk_ref[hh])
        q_ref[:, hh * QK_DIM:hh * QK_DIM + KV_LORA] = (ql * SOFTMAX_SCALE).astype(BF16)
        qr = (q[:, rope_lo + hh * LANES:rope_lo + (hh + 1) * LANES] * cos
              + q[:, swap_lo + hh * LANES:swap_lo + (hh + 1) * LANES] * sin)
        q_ref[:, hh * QK_DIM + KV_LORA:(hh + 1) * QK_DIM] = (qr * SOFTMAX_SCALE).astype(BF16)


def _mla_in(x, cos, sin, g, w_in, q_norm, kv_norm, w_qb, w_uk):
    m = x.shape[0]
    d = x.shape[1]
    return pl.pallas_call(
        _mla_in_kernel,
        grid=(m // ROW_TILE,),
        in_specs=[_rows(d), _rows(LANES), _rows(LANES), _resident(g.shape), _resident(w_in.shape),
                  _resident(q_norm.shape), _resident(kv_norm.shape), _resident(w_qb.shape),
                  _resident(w_uk.shape)],
        out_specs=[_rows(MLA_HEADS * QK_DIM), _rows(QK_DIM), _rows(KV_LORA), _rows(ROPE_DIM)],
        out_shape=[jax.ShapeDtypeStruct((m, MLA_HEADS * QK_DIM), BF16),
                   jax.ShapeDtypeStruct((m, QK_DIM), BF16),
                   jax.ShapeDtypeStruct((m, KV_LORA), F32),
                   jax.ShapeDtypeStruct((m, ROPE_DIM), F32)],
        compiler_params=_params("parallel"),
        name="mla_in",
    )(x, cos, sin, g, w_in, q_norm, kv_norm, w_qb, w_uk)


def _attn_prompt_kernel(q_ref, kt_ref, v_ref, o_ref, q_scr, m_scr, l_scr, acc_scr):
    qi = pl.program_id(1)
    tq = ATTN_Q_TOKENS
    rows = tq * MLA_HEADS
    for hh in range(MLA_HEADS):
        q_scr[hh * tq:(hh + 1) * tq, :] = q_ref[:, hh * QK_DIM:(hh + 1) * QK_DIM]
    m_scr[...] = jnp.full(m_scr.shape, MASK_VALUE, F32)
    l_scr[...] = jnp.zeros(l_scr.shape, F32)
    acc_scr[...] = jnp.zeros(acc_scr.shape, F32)

    def process(first_block, n_blocks, diagonal):
        kt = [kt_ref[first_block + n] for n in range(n_blocks)]
        kt = kt[0] if n_blocks == 1 else jnp.concatenate(kt, axis=-1)
        start = pl.multiple_of(first_block * tq, tq)
        v = v_ref[pl.ds(start, n_blocks * tq), :KV_LORA]
        for r0 in range(0, rows, ATTN_ROW_CHUNK):
            sl = slice(r0, r0 + ATTN_ROW_CHUNK)
            s = _dot(q_scr[sl, :], kt)
            if diagonal:
                t_loc = (r0 + lax.broadcasted_iota(jnp.int32, (ATTN_ROW_CHUNK, 1), 0)) % tq
                col = lax.broadcasted_iota(jnp.int32, (1, n_blocks * tq), 1) - (n_blocks - 1) * tq
                s = jnp.where(col <= t_loc, s, MASK_VALUE)
            tiles = [s[:, n * tq:(n + 1) * tq] for n in range(n_blocks)]
            tile_max = functools.reduce(jnp.maximum, tiles)
            m_prev = m_scr[sl, :]
            m_new = jnp.maximum(m_prev, jnp.max(tile_max, axis=-1, keepdims=True))
            alpha = jnp.exp(m_prev - m_new)
            ps = [jnp.exp(tile - m_new) for tile in tiles]
            l_scr[sl, :] = alpha * l_scr[sl, :] + functools.reduce(jnp.add, ps)
            p = ps[0] if n_blocks == 1 else jnp.concatenate(ps, axis=-1)
            alpha_wide = jnp.concatenate([alpha] * (KV_LORA // tq), axis=-1)
            acc_scr[sl, :] = alpha_wide * acc_scr[sl, :] + _dot(p.astype(BF16), v)
            m_scr[sl, :] = m_new

    def quad(kj, carry):
        process(ATTN_K_GROUP * kj, ATTN_K_GROUP, False)
        return carry

    lax.fori_loop(0, qi // ATTN_K_GROUP, quad, 0)
    rest = qi % ATTN_K_GROUP
    for n in range(ATTN_K_GROUP):
        pl.when(rest == n)(functools.partial(process, qi - n, n + 1, True))
    for hh in range(MLA_HEADS):
        sl = slice(hh * tq, (hh + 1) * tq)
        l_row = jnp.sum(l_scr[sl, :], axis=-1, keepdims=True)
        o_ref[:, hh * KV_LORA:(hh + 1) * KV_LORA] = (acc_scr[sl, :] / l_row).astype(BF16)


def _attn_prompt(q, kt, kv, n_seq):
    blocks = kt.shape[1]
    t = kv.shape[1]
    tq = ATTN_Q_TOKENS
    rows = tq * MLA_HEADS
    return pl.pallas_call(
        _attn_prompt_kernel,
        grid=(n_seq, blocks),
        in_specs=[pl.BlockSpec((tq, MLA_HEADS * QK_DIM), lambda i, j: (i * blocks + j, 0)),
                  pl.BlockSpec((None, blocks, QK_DIM, tq), lambda i, j: (i, 0, 0, 0)),
                  pl.BlockSpec((None, t, QK_DIM), lambda i, j: (i, 0, 0))],
        out_specs=pl.BlockSpec((tq, MLA_HEADS * KV_LORA), lambda i, j: (i * blocks + j, 0)),
        out_shape=jax.ShapeDtypeStruct((q.shape[0], MLA_HEADS * KV_LORA), BF16),
        scratch_shapes=[pltpu.VMEM((rows, QK_DIM), BF16), pltpu.VMEM((rows, tq), F32),
                        pltpu.VMEM((rows, tq), F32), pltpu.VMEM((rows, KV_LORA), F32)],
        compiler_params=_params("parallel", "arbitrary"),
        name="attn_prompt",
    )(q, kt, kv)


def _attn_decode_kernel(pt_ref, q_ref, cn_ref, krn_ref, *rest, n_new):
    lat_pages = rest[:DECODE_PAGES]
    rope_pages = rest[DECODE_PAGES:2 * DECODE_PAGES]
    o_ref, m_scr, l_scr, acc_scr = rest[2 * DECODE_PAGES:]
    g = pl.program_id(1)
    q = q_ref[...]
    ql = q[:, :KV_LORA]
    qr = q[:, KV_LORA:KV_LORA + ROPE_DIM]
    rows = q.shape[0]

    @pl.when(g == 0)
    def _():
        qlf = ql.astype(F32)
        qrf = qr.astype(F32)
        cn = cn_ref[...].astype(BF16).astype(F32)
        krn = krn_ref[...].astype(BF16).astype(F32)
        q_tok = lax.broadcasted_iota(jnp.int32, (rows, 1), 0) % n_new
        cols = []
        for t in range(n_new):
            s_t = (jnp.sum(qlf * cn[t:t + 1, :], axis=-1, keepdims=True)
                   + jnp.sum(qrf * krn[t:t + 1, :], axis=-1, keepdims=True))
            cols.append(jnp.where(q_tok >= t, s_t, MASK_VALUE))
        m0 = cols[0]
        for t in range(1, n_new):
            m0 = jnp.maximum(m0, cols[t])
        l0 = jnp.zeros((rows, 1), F32)
        acc0 = jnp.zeros((rows, KV_LORA), F32)
        for t in range(n_new):
            p_t = jnp.exp(cols[t] - m0)
            l0 = l0 + p_t
            acc0 = acc0 + p_t.astype(BF16).astype(F32) * cn[t:t + 1, :]
        m_scr[...] = m0
        l_scr[...] = l0
        acc_scr[...] = acc0

    per_group = DECODE_PAGES // DECODE_GROUPS
    partials = []
    for grp in range(DECODE_GROUPS):
        idx = range(grp * per_group, (grp + 1) * per_group)
        kbs = [lat_pages[i][...].astype(BF16) for i in idx]
        s = jnp.concatenate(
            [_dot_nt(ql, kb) + _dot(qr, rope_pages[i][...].astype(BF16)) for i, kb in zip(idx, kbs)],
            axis=-1)
        m_g = jnp.max(s, axis=-1, keepdims=True)
        p = jnp.exp(s - m_g)
        l_g = jnp.sum(p, axis=-1, keepdims=True)
        pb = p.astype(BF16)
        acc_g = _dot(pb[:, :PAGE_SIZE], kbs[0])
        for n in range(1, per_group):
            acc_g = acc_g + _dot(pb[:, n * PAGE_SIZE:(n + 1) * PAGE_SIZE], kbs[n])
        partials.append((m_g, l_g, acc_g))

    m_prev = m_scr[...]
    m_new = m_prev
    for m_g, _, _ in partials:
        m_new = jnp.maximum(m_new, m_g)
    alpha = jnp.exp(m_prev - m_new)
    l_new = alpha * l_scr[...]
    acc = alpha * acc_scr[...]
    for m_g, l_g, acc_g in partials:
        w_g = jnp.exp(m_g - m_new)
        l_new = l_new + w_g * l_g
        acc = acc + w_g * acc_g
    l_scr[...] = l_new
    acc_scr[...] = acc
    m_scr[...] = m_new

    @pl.when(g == pl.num_programs(1) - 1)
    def _():
        o_ref[...] = (acc_scr[...] / l_scr[...]).astype(BF16)


def _attn_decode(page_table, q, c_new, kr_new, cache_latent, cache_krope_t):
    db, rows, _ = q.shape
    n_new = c_new.shape[1]
    n_pages = page_table.shape[1]

    def page_spec(shape, i):
        return pl.BlockSpec((None, None) + shape,
                            lambda b, g, pt: (0, pt[b, g * DECODE_PAGES + i], 0, 0))

    grid_spec = pltpu.PrefetchScalarGridSpec(
        num_scalar_prefetch=1,
        grid=(db, n_pages // DECODE_PAGES),
        in_specs=([pl.BlockSpec((None, rows, QK_DIM), lambda b, g, pt: (b, 0, 0)),
                   pl.BlockSpec((None, n_new, KV_LORA), lambda b, g, pt: (b, 0, 0)),
                   pl.BlockSpec((None, n_new, ROPE_DIM), lambda b, g, pt: (b, 0, 0))]
                  + [page_spec((PAGE_SIZE, KV_LORA), i) for i in range(DECODE_PAGES)]
                  + [page_spec((ROPE_DIM, PAGE_SIZE), i) for i in range(DECODE_PAGES)]),
        out_specs=pl.BlockSpec((None, rows, KV_LORA), lambda b, g, pt: (b, 0, 0)),
        scratch_shapes=[pltpu.VMEM((rows, 1), F32), pltpu.VMEM((rows, 1), F32),
                        pltpu.VMEM((rows, KV_LORA), F32)],
    )
    return pl.pallas_call(
        functools.partial(_attn_decode_kernel, n_new=n_new),
        grid_spec=grid_spec,
        out_shape=jax.ShapeDtypeStruct((db, rows, KV_LORA), BF16),
        compiler_params=_params("parallel", "arbitrary"),
        name="attn_decode",
    )(page_table, q, c_new, kr_new,
      *([cache_latent] * DECODE_PAGES), *([cache_krope_t] * DECODE_PAGES))


def _mla_out_kernel(o_ref, x_ref, wuv_ref, wo_ref, out_ref):
    us = [_dot(o_ref[:, hh * KV_LORA:(hh + 1) * KV_LORA], wuv_ref[hh]).astype(BF16)
          for hh in range(MLA_HEADS)]
    out_ref[...] = x_ref[...] + _dot(jnp.concatenate(us, axis=-1), wo_ref[...])


def _mla_out(o, x, w_uv, w_o):
    m, d = x.shape
    return pl.pallas_call(
        _mla_out_kernel,
        grid=(m // ROW_TILE,),
        in_specs=[_rows(o.shape[1]), _rows(d), _resident(w_uv.shape), _resident(w_o.shape)],
        out_specs=_rows(d),
        out_shape=jax.ShapeDtypeStruct((m, d), F32),
        compiler_params=_params("parallel"),
        name="mla_out",
    )(o, x, w_uv, w_o)


def _ffn_kernel(x_ref, g_ref, wgu_ref, wd_ref, gn_ref, x_out_ref, hn_ref):
    x = x_ref[...]
    h = _rms(x, g_ref[...]).astype(BF16)
    acc = x
    for ci in range(wgu_ref.shape[0]):
        gu = _dot(h, wgu_ref[ci])
        gate = gu[:, :FFN_CHUNK]
        act = (gate * _sigmoid(gate) * gu[:, FFN_CHUNK:]).astype(BF16)
        acc = acc + _dot(act, wd_ref[ci])
    x_out_ref[...] = acc
    hn_ref[...] = _rms(acc, gn_ref[...])


def _ffn(x, g, wgu, wd, g_next):
    m, d = x.shape
    return pl.pallas_call(
        _ffn_kernel,
        grid=(m // ROW_TILE,),
        in_specs=[_rows(d), _resident(g.shape), _resident(wgu.shape), _resident(wd.shape),
                  _resident(g_next.shape)],
        out_specs=[_rows(d), _rows(d)],
        out_shape=[jax.ShapeDtypeStruct((m, d), F32), jax.ShapeDtypeStruct((m, d), F32)],
        compiler_params=_params("parallel"),
        name="ffn",
    )(x, g, wgu, wd, g_next)


def _time_mix_kernel(h_ref, xp_ref, mix_ref, wr_ref, wk_ref, wv_ref, w0_ref, w1_ref, w2_ref,
                     a0_ref, a1_ref, a2_ref, g1_ref, g2_ref,
                     r_ref, w_ref, k_ref, v_ref, a_ref, g_ref):
    h = h_ref[...]
    xx = xp_ref[...] - h
    mix = mix_ref[...]

    def mixed(i):
        return (h + xx * mix[i:i + 1, :]).astype(BF16)

    r_ref[...] = _dot(mixed(0), wr_ref[...])
    lw = jnp.tanh(_dot(mixed(1), w1_ref[...])).astype(BF16)
    z = -(w0_ref[...] + _dot(lw, w2_ref[...]))
    softplus = jnp.maximum(z, 0.0) + jnp.log(1.0 + jnp.exp(-jnp.abs(z)))
    w_ref[...] = jnp.exp(-jnp.exp(-softplus - 0.5))
    k_ref[...] = _dot(mixed(2), wk_ref[...])
    v_ref[...] = _dot(mixed(3), wv_ref[...])
    al = _dot(mixed(4), a1_ref[...]).astype(BF16)
    a_ref[...] = _sigmoid(a0_ref[...] + _dot(al, a2_ref[...]))
    gl = _sigmoid(_dot(mixed(5), g1_ref[...])).astype(BF16)
    g_ref[...] = _dot(gl, g2_ref[...])


def _time_mix(h, xprev, mix, w_r, w_k, w_v, w0, w1, w2, a0, a1, a2, g1, g2):
    m, d = h.shape
    weights = (mix, w_r, w_k, w_v, w0, w1, w2, a0, a1, a2, g1, g2)
    return pl.pallas_call(
        _time_mix_kernel,
        grid=(m // ROW_TILE,),
        in_specs=[_rows(d), _rows(d)] + [_resident(w.shape) for w in weights],
        out_specs=[_rows(d)] * 6,
        out_shape=[jax.ShapeDtypeStruct((m, d), F32)] * 6,
        compiler_params=_params("parallel"),
        name="time_mix",
    )(h, xprev, *weights)


def _wkv_kernel(r_ref, w_ref, k_ref, v_ref, a_ref, s0_ref, kk_ref, ka_ref, rk_ref, lnw_ref, lnb_ref,
                y_ref, st_ref, s_scr, y_scr):
    step_block = pl.program_id(1)
    n = s_scr.shape[0]

    @pl.when(step_block == 0)
    def _():
        s_scr[...] = s0_ref[...]

    k_k = kk_ref[...]
    k_a = ka_ref[...]
    r_k = rk_ref[...]
    lnw = lnw_ref[...]
    lnb = lnb_ref[...]

    def step(t, carry):
        r = r_ref[t]
        w = w_ref[t]
        k = k_ref[t]
        a = a_ref[t]
        kk = k * k_k
        kk = kk / jnp.maximum(jnp.sqrt(jnp.sum(kk * kk, axis=0, keepdims=True)), 1e-12)
        b = kk * a
        k = k * (1.0 + (a - 1.0) * k_a)

        def row(i, c2):
            s_i = s_scr[i]
            sa = -jnp.sum(s_i * kk, axis=0, keepdims=True)
            v_i = v_ref[t, pl.ds(i, 1), :]
            s_i = s_i * w + sa * b + v_i * k
            s_scr[i] = s_i
            y_scr[pl.ds(i, 1), :] = jnp.sum(s_i * r, axis=0, keepdims=True)
            return c2

        lax.fori_loop(0, n, row, 0, unroll=8)
        y = y_scr[...]
        mu = jnp.mean(y, axis=0, keepdims=True)
        yc = y - mu
        var = jnp.mean(yc * yc, axis=0, keepdims=True)
        bonus = jnp.sum(r * k * r_k, axis=0, keepdims=True) * v_ref[t]
        y_ref[t] = yc * lax.rsqrt(var + LNX_EPS) * lnw + lnb + bonus
        return carry

    lax.fori_loop(0, r_ref.shape[0], step, 0)

    @pl.when(step_block == pl.num_programs(1) - 1)
    def _():
        st_ref[...] = s_scr[...]


def _wkv(r, w, k, v, a, s0, k_k, k_a, r_k, lnw, lnb, steps_per_block):
    t, n, lanes = r.shape
    seq = pl.BlockSpec((steps_per_block, n, WKV_LANES), lambda i, j: (j, 0, i))
    state = pl.BlockSpec((n, n, WKV_LANES), lambda i, j: (0, 0, i))
    table = pl.BlockSpec((n, WKV_LANES), lambda i, j: (0, i))
    return pl.pallas_call(
        _wkv_kernel,
        grid=(lanes // WKV_LANES, t // steps_per_block),
        in_specs=[seq] * 5 + [state] + [table] * 5,
        out_specs=[seq, state],
        out_shape=[jax.ShapeDtypeStruct((t, n, lanes), F32), jax.ShapeDtypeStruct((n, n, lanes), F32)],
        scratch_shapes=[pltpu.VMEM((n, n, WKV_LANES), F32), pltpu.VMEM((n, WKV_LANES), F32)],
        compiler_params=_params("parallel", "arbitrary"),
        name="wkv",
    )(r, w, k, v, a, s0, k_k, k_a, r_k, lnw, lnb)


def _rwkv_out_kernel(y_ref, g_ref, x_ref, wo_ref, out_ref):
    out_ref[...] = x_ref[...] + _dot((y_ref[...] * g_ref[...]).astype(BF16), wo_ref[...])


def _rwkv_out(y, g, x, w_o):
    m, d = x.shape
    return pl.pallas_call(
        _rwkv_out_kernel,
        grid=(m // ROW_TILE,),
        in_specs=[_rows(d), _rows(d), _rows(d), _resident(w_o.shape)],
        out_specs=_rows(d),
        out_shape=jax.ShapeDtypeStruct((m, d), F32),
        compiler_params=_params("parallel"),
        name="rwkv_out",
    )(y, g, x, w_o)


def _rope_tables(pos, reps):
    half = ROPE_DIM // 2
    inv = ROPE_THETA ** (-jnp.arange(half, dtype=F32) / half)
    ang = pos.astype(F32)[:, None] * inv[None, :]
    cos, sin = jnp.cos(ang), jnp.sin(ang)
    cos = jnp.tile(jnp.concatenate([cos, cos], axis=-1), (reps, LANES // ROPE_DIM))
    sin = jnp.tile(jnp.concatenate([-sin, sin], axis=-1), (reps, LANES // ROPE_DIM))
    return cos, sin


def _swap_halves(w):
    half = w.shape[-1] // 2
    return jnp.concatenate([w[..., half:], w[..., :half]], axis=-1)


def _pad_lanes(w):
    return jnp.pad(w, [(0, 0)] * (w.ndim - 1) + [(0, LANES - w.shape[-1])])


def _to_lanes(x, batch, steps, heads):
    x = x.reshape(batch, -1, heads, RWKV_HEAD)[:, :steps]
    return x.transpose(1, 3, 0, 2).reshape(steps, RWKV_HEAD, batch * heads)


def _from_lanes(y, batch, heads, padded_steps):
    steps = y.shape[0]
    y = y.reshape(steps, RWKV_HEAD, batch, heads).transpose(2, 0, 3, 1)
    y = jnp.pad(y, ((0, 0), (0, padded_steps - steps), (0, 0), (0, 0)))
    return y.reshape(batch * padded_steps, heads * RWKV_HEAD)


def _head_table(p, batch, heads):
    return jnp.tile(p.reshape(heads, RWKV_HEAD).T, (1, batch))


def kernel(x_prompt, x_sample, cache_latent, cache_krope, state_shift, state_wkv, page_table, meta_tokens, ln_mix, ln_ffn, ln_final, attn_w_in, attn_q_norm, attn_kv_norm, attn_w_qb, attn_w_uk, attn_w_uv, attn_w_o, rwkv_mix, rwkv_w_r, rwkv_w_k, rwkv_w_v, rwkv_w_o, rwkv_w0, rwkv_w1, rwkv_w2, rwkv_a0, rwkv_a1, rwkv_a2, rwkv_g1, rwkv_g2, rwkv_k_k, rwkv_k_a, rwkv_r_k, rwkv_lnx_w, rwkv_lnx_b, ffn_w_gate, ffn_w_up, ffn_w_down):
    nb, seq, d = x_prompt.shape
    db, ds, _ = x_sample.shape
    t_real = seq + N_META
    t_pad = -(-t_real // SEQ_ALIGN) * SEQ_ALIGN
    past_len = page_table.shape[1] * PAGE_SIZE
    heads = d // RWKV_HEAD
    row = lambda p: p.reshape(1, -1).astype(F32)

    meta = jnp.broadcast_to(meta_tokens[None].astype(x_prompt.dtype), (nb, N_META, d))
    xp = jnp.concatenate([meta, x_prompt, jnp.zeros((nb, t_pad - t_real, d), x_prompt.dtype)], axis=1)
    xp = xp.reshape(nb * t_pad, d)
    xs = x_sample.reshape(db * ds, d)
    cos_p, sin_p = _rope_tables(jnp.arange(t_pad), nb)
    cos_s, sin_s = _rope_tables(past_len + jnp.arange(ds), db)

    w_in = attn_w_in[0]
    kr_cols = w_in[:, Q_LORA + KV_LORA:]
    w_in_ext = jnp.concatenate(
        [w_in[:, :Q_LORA + KV_LORA], _pad_lanes(kr_cols), _pad_lanes(_swap_halves(kr_cols))],
        axis=-1).astype(BF16)
    w_qb = attn_w_qb[0].reshape(Q_LORA, MLA_HEADS, NOPE_DIM + ROPE_DIM)
    q_rope_cols = w_qb[:, :, NOPE_DIM:]
    w_qb_ext = jnp.concatenate(
        [w_qb[:, :, :NOPE_DIM].reshape(Q_LORA, -1), _pad_lanes(q_rope_cols).reshape(Q_LORA, -1),
         _pad_lanes(_swap_halves(q_rope_cols)).reshape(Q_LORA, -1)], axis=-1).astype(BF16)
    w_uk_t = attn_w_uk[0].transpose(1, 2, 0).astype(BF16)
    w_uv = attn_w_uv[0].transpose(1, 0, 2).astype(BF16)
    w_o = attn_w_o[0].astype(BF16)
    mla_w = (row(ln_mix[0]), w_in_ext, row(attn_q_norm[0]), row(attn_kv_norm[0]), w_qb_ext, w_uk_t)

    q_p, kv_p, c_p, kr_p = _mla_in(xp, cos_p, sin_p, *mla_w)
    q_s, _, c_s, kr_s = _mla_in(xs, cos_s, sin_s, *mla_w)

    kv3 = kv_p.reshape(nb, t_pad, QK_DIM)
    kt4 = kv3.reshape(nb, t_pad // ATTN_Q_TOKENS, ATTN_Q_TOKENS, QK_DIM).transpose(0, 1, 3, 2)
    o_p = _attn_prompt(q_p, kt4, kv3, nb)
    q_s3 = q_s.reshape(db, ds, MLA_HEADS, QK_DIM).transpose(0, 2, 1, 3).reshape(db, MLA_HEADS * ds, QK_DIM)
    o_s = _attn_decode(page_table, q_s3, c_s.reshape(db, ds, KV_LORA), kr_s.reshape(db, ds, ROPE_DIM),
                       cache_latent[:1], jnp.swapaxes(cache_krope[:1], 2, 3))
    o_s = o_s.reshape(db, MLA_HEADS, ds, KV_LORA).transpose(0, 2, 1, 3).reshape(db * ds, MLA_HEADS * KV_LORA)

    def ffn_weights(i):
        n_chunks = ffn_w_gate.shape[2] // FFN_CHUNK
        wg = ffn_w_gate[i].reshape(d, n_chunks, FFN_CHUNK)
        wu = ffn_w_up[i].reshape(d, n_chunks, FFN_CHUNK)
        wgu = jnp.concatenate([wg, wu], axis=-1).transpose(1, 0, 2).astype(BF16)
        wd = ffn_w_down[i].reshape(n_chunks, FFN_CHUNK, d).astype(BF16)
        return wgu, wd

    ffn0 = (row(ln_ffn[0]),) + ffn_weights(0) + (row(ln_mix[1]),)
    xp, hp = _ffn(_mla_out(o_p, xp, w_uv, w_o), *ffn0)
    xs, hs = _ffn(_mla_out(o_s, xs, w_uv, w_o), *ffn0)

    hp3 = hp.reshape(nb, t_pad, d)
    hs3 = hs.reshape(db, ds, d)
    prev_p = jnp.concatenate([jnp.zeros((nb, 1, d), F32), hp3[:, :-1]], axis=1).reshape(nb * t_pad, d)
    prev_s = jnp.concatenate([state_shift[0][:, None].astype(F32), hs3[:, :-1]], axis=1).reshape(db * ds, d)

    def lora(w_down, w_up):
        rank = w_down.shape[1]
        pad = -(-rank // LANES) * LANES - rank
        return (jnp.pad(w_down, ((0, 0), (0, pad))).astype(BF16),
                jnp.pad(w_up, ((0, pad), (0, 0))).astype(BF16))

    w1, w2 = lora(rwkv_w1[0], rwkv_w2[0])
    a1, a2 = lora(rwkv_a1[0], rwkv_a2[0])
    g1, g2 = lora(rwkv_g1[0], rwkv_g2[0])
    tm_w = (rwkv_mix[0].astype(F32), rwkv_w_r[0].astype(BF16), rwkv_w_k[0].astype(BF16),
            rwkv_w_v[0].astype(BF16), row(rwkv_w0[0]), w1, w2, row(rwkv_a0[0]), a1, a2, g1, g2)
    tables = (rwkv_k_k[0], rwkv_k_a[0], rwkv_r_k[0].reshape(-1), rwkv_lnx_w[0], rwkv_lnx_b[0])
    w_o1 = rwkv_w_o[0].astype(BF16)

    def time_mixing(h, prev, x, batch, steps, padded_steps, s0, steps_per_block):
        r, w, k, v, a, g = _time_mix(h, prev, *tm_w)
        seqs = [_to_lanes(u, batch, steps, heads) for u in (r, w, k, v, a)]
        tabs = [_head_table(p.astype(F32), batch, heads) for p in tables]
        y, s_t = _wkv(*seqs, s0, *tabs, steps_per_block)
        y = _from_lanes(y, batch, heads, padded_steps)
        s_t = s_t.reshape(RWKV_HEAD, RWKV_HEAD, batch, heads).transpose(2, 3, 0, 1)
        return _rwkv_out(y, g, x, w_o1), s_t

    s0_p = jnp.zeros((RWKV_HEAD, RWKV_HEAD, nb * heads), F32)
    s0_s = state_wkv[0].astype(F32).transpose(2, 3, 0, 1).reshape(RWKV_HEAD, RWKV_HEAD, db * heads)
    xp, wkv_p = time_mixing(hp, prev_p, xp, nb, t_real, t_pad, s0_p, WKV_STEPS)
    xs, wkv_s = time_mixing(hs, prev_s, xs, db, ds, ds, s0_s, ds)

    ffn1 = (row(ln_ffn[1]),) + ffn_weights(1) + (row(ln_final),)
    _, yp = _ffn(xp, *ffn1)
    _, ys = _ffn(xs, *ffn1)

    seq3 = lambda u: u.reshape(nb, t_pad, -1)[:, :t_real]
    return (yp.reshape(nb, t_pad, d)[:, N_META:t_real],
            ys.reshape(db, ds, d),
            seq3(c_p)[None],
            seq3(kr_p)[None],
            hp3[:, t_real - 1][None],
            wkv_p.astype(state_wkv.dtype)[None],
            c_s.reshape(db, ds, KV_LORA)[None],
            kr_s.reshape(db, ds, ROPE_DIM)[None],
            hs3[:, -1][None],
            wkv_s.astype(state_wkv.dtype)[None])
```

```python
import functools

import jax
import jax.numpy as jnp
from jax import lax
from jax.experimental import pallas as pl
from jax.experimental.pallas import tpu as pltpu

F32 = jnp.float32
BF16 = jnp.bfloat16

N_META = 16
RMS_EPS = 1e-6
MLA_HEADS = 8
Q_LORA = 512
KV_LORA = 256
NOPE_DIM = 128
ROPE_DIM = 64
V_DIM = 128
ROPE_THETA = 10000.0
SOFTMAX_SCALE = (NOPE_DIM + ROPE_DIM) ** -0.5
PAGE_SIZE = 128
RWKV_HEAD = 64
LNX_EPS = 64e-5
MASK_VALUE = -1e30

LANES = 128
QK_DIM = KV_LORA + LANES
ROW_TILE = 256
SEQ_ALIGN = 128
ATTN_Q_TOKENS = 128
ATTN_ROW_CHUNK = 512
ATTN_K_GROUP = 4
DECODE_PAGES = 32
DECODE_GROUPS = 2
FFN_CHUNK = 256
WKV_LANES = 128
WKV_STEPS = 16
VMEM_LIMIT = 56 * 1024 * 1024


def _params(*sem):
    return pltpu.CompilerParams(dimension_semantics=sem, vmem_limit_bytes=VMEM_LIMIT)


def _resident(shape):
    nd = len(shape)
    return pl.BlockSpec(shape, lambda *_: (0,) * nd, pipeline_mode=pl.Buffered(1))


def _rows(width, tile=ROW_TILE):
    return pl.BlockSpec((tile, width), lambda i: (i, 0))


def _rms(x, g):
    return x * lax.rsqrt(jnp.mean(x * x, axis=-1, keepdims=True) + RMS_EPS) * g


def _sigmoid(x):
    return 1.0 / (1.0 + jnp.exp(-x))


def _dot(a, b):
    return jnp.dot(a, b, preferred_element_type=F32)


def _dot_nt(a, b):
    return lax.dot_general(a, b, (((1,), (1,)), ((), ())), preferred_element_type=F32)


def _mla_in_kernel(x_ref, cos_ref, sin_ref, g_ref, win_ref, qn_ref, kvn_ref, wqb_ref, wuk_ref,
                   q_ref, kv_ref, c_ref, kr_ref):
    h = _rms(x_ref[...], g_ref[...]).astype(BF16)
    z = _dot(h, win_ref[...])
    cos = cos_ref[...]
    sin = sin_ref[...]
    c = _rms(z[:, Q_LORA:Q_LORA + KV_LORA], kvn_ref[...])
    kr_lo = Q_LORA + KV_LORA
    kr = z[:, kr_lo:kr_lo + LANES] * cos + z[:, kr_lo + LANES:kr_lo + 2 * LANES] * sin
    c_ref[...] = c
    kr_ref[...] = kr[:, :ROPE_DIM]
    kv_ref[:, :KV_LORA] = c.astype(BF16)
    kv_ref[:, KV_LORA:] = kr.astype(BF16)
    qn = _rms(z[:, :Q_LORA], qn_ref[...]).astype(BF16)
    q = _dot(qn, wqb_ref[...])
    rope_lo = MLA_HEADS * NOPE_DIM
    swap_lo = rope_lo + MLA_HEADS * LANES
    for hh in range(MLA_HEADS):
        ql = _dot(q[:, hh * NOPE_DIM:(hh + 1) * NOPE_DIM].astype(BF16), wuk_ref[hh])
        q_ref[:, hh * QK_DIM:hh * QK_DIM + KV_LORA] = (ql * SOFTMAX_SCALE).astype(BF16)
        qr = (q[:, rope_lo + hh * LANES:rope_lo + (hh + 1) * LANES] * cos
              + q[:, swap_lo + hh * LANES:swap_lo + (hh + 1) * LANES] * sin)
        q_ref[:, hh * QK_DIM + KV_LORA:(hh + 1) * QK_DIM] = (qr * SOFTMAX_SCALE).astype(BF16)


def _mla_in(x, cos, sin, g, w_in, q_norm, kv_norm, w_qb, w_uk):
    m = x.shape[0]
    d = x.shape[1]
    return pl.pallas_call(
        _mla_in_kernel,
        grid=(m // ROW_TILE,),
        in_specs=[_rows(d), _rows(LANES), _rows(LANES), _resident(g.shape), _resident(w_in.shape),
                  _resident(q_norm.shape), _resident(kv_norm.shape), _resident(w_qb.shape),
                  _resident(w_uk.shape)],
        out_specs=[_rows(MLA_HEADS * QK_DIM), _rows(QK_DIM), _rows(KV_LORA), _rows(ROPE_DIM)],
        out_shape=[jax.ShapeDtypeStruct((m, MLA_HEADS * QK_DIM), BF16),
                   jax.ShapeDtypeStruct((m, QK_DIM), BF16),
                   jax.ShapeDtypeStruct((m, KV_LORA), F32),
                   jax.ShapeDtypeStruct((m, ROPE_DIM), F32)],
        compiler_params=_params("parallel"),
        name="mla_in",
    )(x, cos, sin, g, w_in, q_norm, kv_norm, w_qb, w_uk)


def _attn_prompt_kernel(q_ref, kt_ref, v_ref, o_ref, q_scr, m_scr, l_scr, acc_scr):
    qi = pl.program_id(1)
    tq = ATTN_Q_TOKENS
    rows = tq * MLA_HEADS
    for hh in range(MLA_HEADS):
        q_scr[hh * tq:(hh + 1) * tq, :] = q_ref[:, hh * QK_DIM:(hh + 1) * QK_DIM]
    m_scr[...] = jnp.full(m_scr.shape, MASK_VALUE, F32)
    l_scr[...] = jnp.zeros(l_scr.shape, F32)
    acc_scr[...] = jnp.zeros(acc_scr.shape, F32)

    def process(first_block, n_blocks, diagonal):
        kt = [kt_ref[first_block + n] for n in range(n_blocks)]
        kt = kt[0] if n_blocks == 1 else jnp.concatenate(kt, axis=-1)
        start = pl.multiple_of(first_block * tq, tq)
        v = v_ref[pl.ds(start, n_blocks * tq), :KV_LORA]
        for r0 in range(0, rows, ATTN_ROW_CHUNK):
            sl = slice(r0, r0 + ATTN_ROW_CHUNK)
            s = _dot(q_scr[sl, :], kt)
            if diagonal:
                t_loc = (r0 + lax.broadcasted_iota(jnp.int32, (ATTN_ROW_CHUNK, 1), 0)) % tq
                col = lax.broadcasted_iota(jnp.int32, (1, n_blocks * tq), 1) - (n_blocks - 1) * tq
                s = jnp.where(col <= t_loc, s, MASK_VALUE)
            tiles = [s[:, n * tq:(n + 1) * tq] for n in range(n_blocks)]
            tile_max = functools.reduce(jnp.maximum, tiles)
            m_prev = m_scr[sl, :]
            m_new = jnp.maximum(m_prev, jnp.max(tile_max, axis=-1, keepdims=True))
            alpha = jnp.exp(m_prev - m_new)
            ps = [jnp.exp(tile - m_new) for tile in tiles]
            l_scr[sl, :] = alpha * l_scr[sl, :] + functools.reduce(jnp.add, ps)
            p = ps[0] if n_blocks == 1 else jnp.concatenate(ps, axis=-1)
            alpha_wide = jnp.concatenate([alpha] * (KV_LORA // tq), axis=-1)
            acc_scr[sl, :] = alpha_wide * acc_scr[sl, :] + _dot(p.astype(BF16), v)
            m_scr[sl, :] = m_new

    def quad(kj, carry):
        process(ATTN_K_GROUP * kj, ATTN_K_GROUP, False)
        return carry

    lax.fori_loop(0, qi // ATTN_K_GROUP, quad, 0)
    rest = qi % ATTN_K_GROUP
    for n in range(ATTN_K_GROUP):
        pl.when(rest == n)(functools.partial(process, qi - n, n + 1, True))
    for hh in range(MLA_HEADS):
        sl = slice(hh * tq, (hh + 1) * tq)
        l_row = jnp.sum(l_scr[sl, :], axis=-1, keepdims=True)
        o_ref[:, hh * KV_LORA:(hh + 1) * KV_LORA] = (acc_scr[sl, :] / l_row).astype(BF16)


def _attn_prompt(q, kt, kv, n_seq):
    blocks = kt.shape[1]
    t = kv.shape[1]
    tq = ATTN_Q_TOKENS
    rows = tq * MLA_HEADS
    return pl.pallas_call(
        _attn_prompt_kernel,
        grid=(n_seq, blocks),
        in_specs=[pl.BlockSpec((tq, MLA_HEADS * QK_DIM), lambda i, j: (i * blocks + j, 0)),
                  pl.BlockSpec((None, blocks, QK_DIM, tq), lambda i, j: (i, 0, 0, 0)),
                  pl.BlockSpec((None, t, QK_DIM), lambda i, j: (i, 0, 0))],
        out_specs=pl.BlockSpec((tq, MLA_HEADS * KV_LORA), lambda i, j: (i * blocks + j, 0)),
        out_shape=jax.ShapeDtypeStruct((q.shape[0], MLA_HEADS * KV_LORA), BF16),
        scratch_shapes=[pltpu.VMEM((rows, QK_DIM), BF16), pltpu.VMEM((rows, tq), F32),
                        pltpu.VMEM((rows, tq), F32), pltpu.VMEM((rows, KV_LORA), F32)],
        compiler_params=_params("parallel", "arbitrary"),
        name="attn_prompt",
    )(q, kt, kv)


def _attn_decode_kernel(pt_ref, q_ref, cn_ref, krn_ref, *rest, n_new):
    lat_pages = rest[:DECODE_PAGES]
    rope_pages = rest[DECODE_PAGES:2 * DECODE_PAGES]
    o_ref, m_scr, l_scr, acc_scr = rest[2 * DECODE_PAGES:]
    g = pl.program_id(1)
    q = q_ref[...]
    ql = q[:, :KV_LORA]
    qr = q[:, KV_LORA:KV_LORA + ROPE_DIM]
    rows = q.shape[0]

    @pl.when(g == 0)
    def _():
        qlf = ql.astype(F32)
        qrf = qr.astype(F32)
        cn = cn_ref[...].astype(BF16).astype(F32)
        krn = krn_ref[...].astype(BF16).astype(F32)
        q_tok = lax.broadcasted_iota(jnp.int32, (rows, 1), 0) % n_new
        cols = []
        for t in range(n_new):
            s_t = (jnp.sum(qlf * cn[t:t + 1, :], axis=-1, keepdims=True)
                   + jnp.sum(qrf * krn[t:t + 1, :], axis=-1, keepdims=True))
            cols.append(jnp.where(q_tok >= t, s_t, MASK_VALUE))
        m0 = cols[0]
        for t in range(1, n_new):
            m0 = jnp.maximum(m0, cols[t])
        l0 = jnp.zeros((rows, 1), F32)
        acc0 = jnp.zeros((rows, KV_LORA), F32)
        for t in range(n_new):
            p_t = jnp.exp(cols[t] - m0)
            l0 = l0 + p_t
            acc0 = acc0 + p_t.astype(BF16).astype(F32) * cn[t:t + 1, :]
        m_scr[...] = m0
        l_scr[...] = l0
        acc_scr[...] = acc0

    per_group = DECODE_PAGES // DECODE_GROUPS
    partials = []
    for grp in range(DECODE_GROUPS):
        idx = range(grp * per_group, (grp + 1) * per_group)
        kbs = [lat_pages[i][...].astype(BF16) for i in idx]
        s = jnp.concatenate(
            [_dot_nt(ql, kb) + _dot(qr, rope_pages[i][...].astype(BF16)) for i, kb in zip(idx, kbs)],
            axis=-1)
        m_g = jnp.max(s, axis=-1, keepdims=True)
        p = jnp.exp(s - m_g)
        l_g = jnp.sum(p, axis=-1, keepdims=True)
        pb = p.astype(BF16)
        acc_g = _dot(pb[:, :PAGE_SIZE], kbs[0])
        for n in range(1, per_group):
            acc_g = acc_g + _dot(pb[:, n * PAGE_SIZE:(n + 1) * PAGE_SIZE], kbs[n])
        partials.append((m_g, l_g, acc_g))

    m_prev = m_scr[...]
    m_new = m_prev
    for m_g, _, _ in partials:
        m_new = jnp.maximum(m_new, m_g)
    alpha = jnp.exp(m_prev - m_new)
    l_new = alpha * l_scr[...]
    acc = alpha * acc_scr[...]
    for m_g, l_g, acc_g in partials:
        w_g = jnp.exp(m_g - m_new)
        l_new = l_new + w_g * l_g
        acc = acc + w_g * acc_g
    l_scr[...] = l_new
    acc_scr[...] = acc
    m_scr[...] = m_new

    @pl.when(g == pl.num_programs(1) - 1)
    def _():
        o_ref[...] = (acc_scr[...] / l_scr[...]).astype(BF16)


def _attn_decode(page_table, q, c_new, kr_new, cache_latent, cache_krope_t):
    db, rows, _ = q.shape
    n_new = c_new.shape[1]
    n_pages = page_table.shape[1]

    def page_spec(shape, i):
        return pl.BlockSpec((None, None) + shape,
                            lambda b, g, pt: (0, pt[b, g * DECODE_PAGES + i], 0, 0))

    grid_spec = pltpu.PrefetchScalarGridSpec(
        num_scalar_prefetch=1,
        grid=(db, n_pages // DECODE_PAGES),
        in_specs=([pl.BlockSpec((None, rows, QK_DIM), lambda b, g, pt: (b, 0, 0)),
                   pl.BlockSpec((None, n_new, KV_LORA), lambda b, g, pt: (b, 0, 0)),
                   pl.BlockSpec((None, n_new, ROPE_DIM), lambda b, g, pt: (b, 0, 0))]
                  + [page_spec((PAGE_SIZE, KV_LORA), i) for i in range(DECODE_PAGES)]
                  + [page_spec((ROPE_DIM, PAGE_SIZE), i) for i in range(DECODE_PAGES)]),
        out_specs=pl.BlockSpec((None, rows, KV_LORA), lambda b, g, pt: (b, 0, 0)),
        scratch_shapes=[pltpu.VMEM((rows, 1), F32), pltpu.VMEM((rows, 1), F32),
                        pltpu.VMEM((rows, KV_LORA), F32)],
    )
    return pl.pallas_call(
        functools.partial(_attn_decode_kernel, n_new=n_new),
        grid_spec=grid_spec,
        out_shape=jax.ShapeDtypeStruct((db, rows, KV_LORA), BF16),
        compiler_params=_params("parallel", "arbitrary"),
        name="attn_decode",
    )(page_table, q, c_new, kr_new,
      *([cache_latent] * DECODE_PAGES), *([cache_krope_t] * DECODE_PAGES))


def _mla_out_kernel(o_ref, x_ref, wuv_ref, wo_ref, out_ref):
    us = [_dot(o_ref[:, hh * KV_LORA:(hh + 1) * KV_LORA], wuv_ref[hh]).astype(BF16)
          for hh in range(MLA_HEADS)]
    out_ref[...] = x_ref[...] + _dot(jnp.concatenate(us, axis=-1), wo_ref[...])


def _mla_out(o, x, w_uv, w_o):
    m, d = x.shape
    return pl.pallas_call(
        _mla_out_kernel,
        grid=(m // ROW_TILE,),
        in_specs=[_rows(o.shape[1]), _rows(d), _resident(w_uv.shape), _resident(w_o.shape)],
        out_specs=_rows(d),
        out_shape=jax.ShapeDtypeStruct((m, d), F32),
        compiler_params=_params("parallel"),
        name="mla_out",
    )(o, x, w_uv, w_o)


def _ffn_kernel(x_ref, g_ref, wgu_ref, wd_ref, gn_ref, x_out_ref, hn_ref):
    x = x_ref[...]
    h = _rms(x, g_ref[...]).astype(BF16)
    acc = x
    for ci in range(wgu_ref.shape[0]):
        gu = _dot(h, wgu_ref[ci])
        gate = gu[:, :FFN_CHUNK]
        act = (gate * _sigmoid(gate) * gu[:, FFN_CHUNK:]).astype(BF16)
        acc = acc + _dot(act, wd_ref[ci])
    x_out_ref[...] = acc
    hn_ref[...] = _rms(acc, gn_ref[...])


def _ffn(x, g, wgu, wd, g_next):
    m, d = x.shape
    return pl.pallas_call(
        _ffn_kernel,
        grid=(m // ROW_TILE,),
        in_specs=[_rows(d), _resident(g.shape), _resident(wgu.shape), _resident(wd.shape),
                  _resident(g_next.shape)],
        out_specs=[_rows(d), _rows(d)],
        out_shape=[jax.ShapeDtypeStruct((m, d), F32), jax.ShapeDtypeStruct((m, d), F32)],
        compiler_params=_params("parallel"),
        name="ffn",
    )(x, g, wgu, wd, g_next)


def _time_mix_kernel(h_ref, xp_ref, mix_ref, wr_ref, wk_ref, wv_ref, w0_ref, w1_ref, w2_ref,
                     a0_ref, a1_ref, a2_ref, g1_ref, g2_ref,
                     r_ref, w_ref, k_ref, v_ref, a_ref, g_ref):
    h = h_ref[...]
    xx = xp_ref[...] - h
    mix = mix_ref[...]

    def mixed(i):
        return (h + xx * mix[i:i + 1, :]).astype(BF16)

    def store_pairs(ref, val):
        for p in range(ref.shape[0]):
            ref[p] = val[:, p * LANES:(p + 1) * LANES]

    store_pairs(r_ref, _dot(mixed(0), wr_ref[...]))
    lw = jnp.tanh(_dot(mixed(1), w1_ref[...])).astype(BF16)
    z = -(w0_ref[...] + _dot(lw, w2_ref[...]))
    softplus = jnp.maximum(z, 0.0) + jnp.log(1.0 + jnp.exp(-jnp.abs(z)))
    store_pairs(w_ref, jnp.exp(-jnp.exp(-softplus - 0.5)))
    store_pairs(k_ref, _dot(mixed(2), wk_ref[...]))
    store_pairs(v_ref, _dot(mixed(3), wv_ref[...]))
    al = _dot(mixed(4), a1_ref[...]).astype(BF16)
    store_pairs(a_ref, _sigmoid(a0_ref[...] + _dot(al, a2_ref[...])))
    gl = _sigmoid(_dot(mixed(5), g1_ref[...])).astype(BF16)
    g_ref[...] = _dot(gl, g2_ref[...])


def _pair_rows(pairs, tile=ROW_TILE):
    return pl.BlockSpec((pairs, tile, LANES), lambda i: (0, i, 0))


def _time_mix(h, xprev, mix, w_r, w_k, w_v, w0, w1, w2, a0, a1, a2, g1, g2):
    m, d = h.shape
    pairs = d // LANES
    weights = (mix, w_r, w_k, w_v, w0, w1, w2, a0, a1, a2, g1, g2)
    return pl.pallas_call(
        _time_mix_kernel,
        grid=(m // ROW_TILE,),
        in_specs=[_rows(d), _rows(d)] + [_resident(w.shape) for w in weights],
        out_specs=[_pair_rows(pairs)] * 5 + [_rows(d)],
        out_shape=[jax.ShapeDtypeStruct((pairs, m, LANES), F32)] * 5 + [jax.ShapeDtypeStruct((m, d), F32)],
        compiler_params=_params("parallel"),
        name="time_mix",
    )(h, xprev, *weights)


def _wkv_kernel(r_ref, w_ref, k_ref, v_ref, a_ref, s0_ref, kk_ref, ka_ref, rk_ref, lnw_ref, lnb_ref,
                y_ref, st_ref, s_scr, v_scr, y_scr, *, rows_per_pair, steps, real_blocks):
    step_block = pl.program_id(1)
    n = s_scr.shape[1]
    pairs = r_ref.shape[0]

    @pl.when(step_block == 0)
    def _():
        s_scr[...] = s0_ref[...]

    def channels_by_lane(ref, off):
        tile = [ref[p, pl.ds(off, rows_per_pair), :] for p in range(pairs)]
        tile = tile[0] if pairs == 1 else jnp.concatenate(tile, axis=0)
        return tile.T

    def step(t, carry):
        off = pl.multiple_of(t * rows_per_pair, rows_per_pair)
        r2 = channels_by_lane(r_ref, off)
        w2 = channels_by_lane(w_ref, off)
        k2 = channels_by_lane(k_ref, off)
        a2 = channels_by_lane(a_ref, off)
        v_scr[...] = channels_by_lane(v_ref, off)
        for half in range(2):
            sl = slice(half * n, (half + 1) * n)
            r, w, k, a = r2[sl], w2[sl], k2[sl], a2[sl]
            kk = k * kk_ref[half]
            kk = kk / jnp.maximum(jnp.sqrt(jnp.sum(kk * kk, axis=0, keepdims=True)), 1e-12)
            b = kk * a
            k = k * (1.0 + (a - 1.0) * ka_ref[half])

            def row(i, c2, half=half, kk=kk, w=w, b=b, k=k, r=r):
                s_i = s_scr[half, i]
                sa = -jnp.sum(s_i * kk, axis=0, keepdims=True)
                v_i = v_scr[pl.ds(half * n + i, 1), :]
                s_i = s_i * w + sa * b + v_i * k
                s_scr[half, i] = s_i
                y_scr[pl.ds(half * n + i, 1), :] = jnp.sum(s_i * r, axis=0, keepdims=True)
                return c2

            lax.fori_loop(0, n, row, 0, unroll=8)
            y = y_scr[sl, :]
            mu = jnp.mean(y, axis=0, keepdims=True)
            yc = y - mu
            var = jnp.mean(yc * yc, axis=0, keepdims=True)
            bonus = jnp.sum(r * k * rk_ref[half], axis=0, keepdims=True) * v_scr[sl, :]
            y_scr[sl, :] = yc * lax.rsqrt(var + LNX_EPS) * lnw_ref[half] + lnb_ref[half] + bonus
        y_rows = y_scr[...].T
        for p in range(pairs):
            y_ref[p, pl.ds(off, rows_per_pair), :] = y_rows[p * rows_per_pair:(p + 1) * rows_per_pair, :]
        return carry

    @pl.when(step_block < real_blocks)
    def _():
        lax.fori_loop(0, steps, step, 0)

    @pl.when(step_block >= real_blocks)
    def _():
        y_ref[...] = jnp.zeros(y_ref.shape, F32)

    @pl.when(step_block == real_blocks - 1)
    def _():
        st_ref[...] = s_scr[...]


def _wkv(r, w, k, v, a, s0, k_k, k_a, r_k, lnw, lnb, batch, real_steps, steps_per_block):
    pairs, rows, _ = r.shape
    n = RWKV_HEAD
    group = WKV_LANES // batch
    block_rows = steps_per_block * batch
    kern = functools.partial(_wkv_kernel, rows_per_pair=batch, steps=steps_per_block,
                             real_blocks=real_steps // steps_per_block)
    seq = pl.BlockSpec((group, block_rows, LANES), lambda i, j: (i, j, 0))
    state = pl.BlockSpec((None, 2, n, n, WKV_LANES), lambda i, j: (i, 0, 0, 0, 0))
    table = pl.BlockSpec((None, 2, n, WKV_LANES), lambda i, j: (i, 0, 0, 0))
    return pl.pallas_call(
        kern,
        grid=(pairs // group, rows // block_rows),
        in_specs=[seq] * 5 + [state] + [table] * 5,
        out_specs=[seq, state],
        out_shape=[jax.ShapeDtypeStruct(r.shape, F32), jax.ShapeDtypeStruct(s0.shape, F32)],
        scratch_shapes=[pltpu.VMEM((2, n, n, WKV_LANES), F32), pltpu.VMEM((2 * n, WKV_LANES), F32),
                        pltpu.VMEM((2 * n, WKV_LANES), F32)],
        compiler_params=_params("parallel", "arbitrary"),
        name="wkv",
    )(r, w, k, v, a, s0, k_k, k_a, r_k, lnw, lnb)


def _rwkv_out_kernel(y_ref, g_ref, x_ref, wo_ref, out_ref):
    y = jnp.concatenate([y_ref[p] for p in range(y_ref.shape[0])], axis=-1)
    out_ref[...] = x_ref[...] + _dot((y * g_ref[...]).astype(BF16), wo_ref[...])


def _rwkv_out(y, g, x, w_o):
    m, d = x.shape
    return pl.pallas_call(
        _rwkv_out_kernel,
        grid=(m // ROW_TILE,),
        in_specs=[_pair_rows(y.shape[0]), _rows(d), _rows(d), _resident(w_o.shape)],
        out_specs=_rows(d),
        out_shape=jax.ShapeDtypeStruct((m, d), F32),
        compiler_params=_params("parallel"),
        name="rwkv_out",
    )(y, g, x, w_o)


def _rope_tables(pos, reps):
    half = ROPE_DIM // 2
    inv = ROPE_THETA ** (-jnp.arange(half, dtype=F32) / half)
    ang = pos.astype(F32)[:, None] * inv[None, :]
    cos, sin = jnp.cos(ang), jnp.sin(ang)
    cos = jnp.tile(jnp.concatenate([cos, cos], axis=-1), (reps, LANES // ROPE_DIM))
    sin = jnp.tile(jnp.concatenate([-sin, sin], axis=-1), (reps, LANES // ROPE_DIM))
    return cos, sin


def _swap_halves(w):
    half = w.shape[-1] // 2
    return jnp.concatenate([w[..., half:], w[..., :half]], axis=-1)


def _pad_lanes(w):
    return jnp.pad(w, [(0, 0)] * (w.ndim - 1) + [(0, LANES - w.shape[-1])])


def _time_major(x, batch):
    d = x.shape[-1]
    return x.reshape(batch, -1, d).transpose(1, 0, 2).reshape(-1, d)


def _pair_table(p, batch):
    pairs = p.shape[0] // LANES
    group = WKV_LANES // batch
    t = p.reshape(pairs // group, group, 2, RWKV_HEAD).transpose(0, 2, 3, 1)
    return jnp.repeat(t, batch, axis=-1)


def _state_to_lanes(s, batch):
    heads, n = s.shape[1], s.shape[2]
    group = WKV_LANES // batch
    s = s.reshape(batch, heads // (2 * group), group, 2, n, n)
    return s.transpose(1, 3, 4, 5, 2, 0).reshape(heads // (2 * group), 2, n, n, WKV_LANES)


def _state_from_lanes(s, batch):
    groups, _, n, _, _ = s.shape
    group = WKV_LANES // batch
    s = s.reshape(groups, 2, n, n, group, batch).transpose(5, 0, 4, 1, 2, 3)
    return s.reshape(batch, groups * group * 2, n, n)


def kernel(x_prompt, x_sample, cache_latent, cache_krope, state_shift, state_wkv, page_table, meta_tokens, ln_mix, ln_ffn, ln_final, attn_w_in, attn_q_norm, attn_kv_norm, attn_w_qb, attn_w_uk, attn_w_uv, attn_w_o, rwkv_mix, rwkv_w_r, rwkv_w_k, rwkv_w_v, rwkv_w_o, rwkv_w0, rwkv_w1, rwkv_w2, rwkv_a0, rwkv_a1, rwkv_a2, rwkv_g1, rwkv_g2, rwkv_k_k, rwkv_k_a, rwkv_r_k, rwkv_lnx_w, rwkv_lnx_b, ffn_w_gate, ffn_w_up, ffn_w_down):
    nb, seq, d = x_prompt.shape
    db, ds, _ = x_sample.shape
    t_real = seq + N_META
    t_pad = -(-t_real // SEQ_ALIGN) * SEQ_ALIGN
    past_len = page_table.shape[1] * PAGE_SIZE
    heads = d // RWKV_HEAD
    row = lambda p: p.reshape(1, -1).astype(F32)

    meta = jnp.broadcast_to(meta_tokens[None].astype(x_prompt.dtype), (nb, N_META, d))
    xp = jnp.concatenate([meta, x_prompt, jnp.zeros((nb, t_pad - t_real, d), x_prompt.dtype)], axis=1)
    xp = xp.reshape(nb * t_pad, d)
    xs = x_sample.reshape(db * ds, d)
    cos_p, sin_p = _rope_tables(jnp.arange(t_pad), nb)
    cos_s, sin_s = _rope_tables(past_len + jnp.arange(ds), db)

    w_in = attn_w_in[0]
    kr_cols = w_in[:, Q_LORA + KV_LORA:]
    w_in_ext = jnp.concatenate(
        [w_in[:, :Q_LORA + KV_LORA], _pad_lanes(kr_cols), _pad_lanes(_swap_halves(kr_cols))],
        axis=-1).astype(BF16)
    w_qb = attn_w_qb[0].reshape(Q_LORA, MLA_HEADS, NOPE_DIM + ROPE_DIM)
    q_rope_cols = w_qb[:, :, NOPE_DIM:]
    w_qb_ext = jnp.concatenate(
        [w_qb[:, :, :NOPE_DIM].reshape(Q_LORA, -1), _pad_lanes(q_rope_cols).reshape(Q_LORA, -1),
         _pad_lanes(_swap_halves(q_rope_cols)).reshape(Q_LORA, -1)], axis=-1).astype(BF16)
    w_uk_t = attn_w_uk[0].transpose(1, 2, 0).astype(BF16)
    w_uv = attn_w_uv[0].transpose(1, 0, 2).astype(BF16)
    w_o = attn_w_o[0].astype(BF16)
    mla_w = (row(ln_mix[0]), w_in_ext, row(attn_q_norm[0]), row(attn_kv_norm[0]), w_qb_ext, w_uk_t)

    q_p, kv_p, c_p, kr_p = _mla_in(xp, cos_p, sin_p, *mla_w)
    q_s, _, c_s, kr_s = _mla_in(xs, cos_s, sin_s, *mla_w)

    kv3 = kv_p.reshape(nb, t_pad, QK_DIM)
    kt4 = kv3.reshape(nb, t_pad // ATTN_Q_TOKENS, ATTN_Q_TOKENS, QK_DIM).transpose(0, 1, 3, 2)
    o_p = _attn_prompt(q_p, kt4, kv3, nb)
    q_s3 = q_s.reshape(db, ds, MLA_HEADS, QK_DIM).transpose(0, 2, 1, 3).reshape(db, MLA_HEADS * ds, QK_DIM)
    o_s = _attn_decode(page_table, q_s3, c_s.reshape(db, ds, KV_LORA), kr_s.reshape(db, ds, ROPE_DIM),
                       cache_latent[:1], jnp.swapaxes(cache_krope[:1], 2, 3))
    o_s = o_s.reshape(db, MLA_HEADS, ds, KV_LORA).transpose(0, 2, 1, 3).reshape(db * ds, MLA_HEADS * KV_LORA)

    def ffn_weights(i):
        n_chunks = ffn_w_gate.shape[2] // FFN_CHUNK
        wg = ffn_w_gate[i].reshape(d, n_chunks, FFN_CHUNK)
        wu = ffn_w_up[i].reshape(d, n_chunks, FFN_CHUNK)
        wgu = jnp.concatenate([wg, wu], axis=-1).transpose(1, 0, 2).astype(BF16)
        wd = ffn_w_down[i].reshape(n_chunks, FFN_CHUNK, d).astype(BF16)
        return wgu, wd

    ffn0 = (row(ln_ffn[0]),) + ffn_weights(0) + (row(ln_mix[1]),)
    xp, hp = _ffn(_mla_out(o_p, xp, w_uv, w_o), *ffn0)
    xs, hs = _ffn(_mla_out(o_s, xs, w_uv, w_o), *ffn0)

    xp, hp, xs, hs = _time_major(xp, nb), _time_major(hp, nb), _time_major(xs, db), _time_major(hs, db)
    prev_p = jnp.concatenate([jnp.zeros((nb, d), F32), hp[:-nb]], axis=0)
    prev_s = jnp.concatenate([state_shift[0].astype(F32), hs[:-db]], axis=0)

    def lora(w_down, w_up):
        rank = w_down.shape[1]
        pad = -(-rank // LANES) * LANES - rank
        return (jnp.pad(w_down, ((0, 0), (0, pad))).astype(BF16),
                jnp.pad(w_up, ((0, pad), (0, 0))).astype(BF16))

    w1, w2 = lora(rwkv_w1[0], rwkv_w2[0])
    a1, a2 = lora(rwkv_a1[0], rwkv_a2[0])
    g1, g2 = lora(rwkv_g1[0], rwkv_g2[0])
    tm_w = (rwkv_mix[0].astype(F32), rwkv_w_r[0].astype(BF16), rwkv_w_k[0].astype(BF16),
            rwkv_w_v[0].astype(BF16), row(rwkv_w0[0]), w1, w2, row(rwkv_a0[0]), a1, a2, g1, g2)
    tables = (rwkv_k_k[0], rwkv_k_a[0], rwkv_r_k[0].reshape(-1), rwkv_lnx_w[0], rwkv_lnx_b[0])
    w_o1 = rwkv_w_o[0].astype(BF16)

    def time_mixing(h, prev, x, batch, steps, s0, steps_per_block):
        r, w, k, v, a, g = _time_mix(h, prev, *tm_w)
        tabs = [_pair_table(p.astype(F32), batch) for p in tables]
        y, s_t = _wkv(r, w, k, v, a, _state_to_lanes(s0, batch), *tabs, batch, steps, steps_per_block)
        return _rwkv_out(y, g, x, w_o1), _state_from_lanes(s_t, batch)

    s0_p = jnp.zeros((nb, heads, RWKV_HEAD, RWKV_HEAD), F32)
    xp, wkv_p = time_mixing(hp, prev_p, xp, nb, t_real, s0_p, WKV_STEPS)
    xs, wkv_s = time_mixing(hs, prev_s, xs, db, ds, state_wkv[0].astype(F32), ds)

    ffn1 = (row(ln_ffn[1]),) + ffn_weights(1) + (row(ln_final),)
    _, yp = _ffn(xp, *ffn1)
    _, ys = _ffn(xs, *ffn1)

    seq3 = lambda u: u.reshape(nb, t_pad, -1)[:, :t_real]
    return (yp.reshape(t_pad, nb, d)[N_META:t_real].transpose(1, 0, 2),
            ys.reshape(ds, db, d).transpose(1, 0, 2),
            seq3(c_p)[None],
            seq3(kr_p)[None],
            hp[(t_real - 1) * nb:t_real * nb][None],
            wkv_p.astype(state_wkv.dtype)[None],
            c_s.reshape(db, ds, KV_LORA)[None],
            kr_s.reshape(db, ds, ROPE_DIM)[None],
            hs[(ds - 1) * db:][None],
            wkv_s.astype(state_wkv.dtype)[None])
```

```python
import functools

import jax
import jax.numpy as jnp
from jax import lax
from jax.experimental import pallas as pl
from jax.experimental.pallas import tpu as pltpu

F32 = jnp.float32
BF16 = jnp.bfloat16

N_META = 16
RMS_EPS = 1e-6
MLA_HEADS = 8
Q_LORA = 512
KV_LORA = 256
NOPE_DIM = 128
ROPE_DIM = 64
V_DIM = 128
ROPE_THETA = 10000.0
SOFTMAX_SCALE = (NOPE_DIM + ROPE_DIM) ** -0.5
PAGE_SIZE = 128
RWKV_HEAD = 64
LNX_EPS = 64e-5
MASK_VALUE = -1e30

LANES = 128
QK_DIM = KV_LORA + LANES
ROW_TILE = 512
SEQ_ALIGN = 128
ATTN_Q_TOKENS = 128
ATTN_ROW_CHUNK = 512
ATTN_K_GROUP = 4
DECODE_PAGES = 32
DECODE_GROUPS = 2
FFN_CHUNK = 256
WKV_LANES = 128
WKV_STEPS = 16
VMEM_LIMIT = 56 * 1024 * 1024


def _params(*sem):
    return pltpu.CompilerParams(dimension_semantics=sem, vmem_limit_bytes=VMEM_LIMIT)


def _resident(shape):
    nd = len(shape)
    return pl.BlockSpec(shape, lambda *_: (0,) * nd, pipeline_mode=pl.Buffered(1))


def _rows(width, tile=ROW_TILE):
    return pl.BlockSpec((tile, width), lambda i: (i, 0))


def _rms(x, g):
    return x * lax.rsqrt(jnp.mean(x * x, axis=-1, keepdims=True) + RMS_EPS) * g


def _sigmoid(x):
    return 1.0 / (1.0 + jnp.exp(-x))


def _dot(a, b):
    return jnp.dot(a, b, preferred_element_type=F32)


def _dot_nt(a, b):
    return lax.dot_general(a, b, (((1,), (1,)), ((), ())), preferred_element_type=F32)


def _mla_in_kernel(x_ref, cos_ref, sin_ref, g_ref, win_ref, qn_ref, kvn_ref, wqb_ref, wuk_ref,
                   q_ref, kv_ref, c_ref, kr_ref):
    h = _rms(x_ref[...], g_ref[...]).astype(BF16)
    z = _dot(h, win_ref[...])
    cos = cos_ref[...]
    sin = sin_ref[...]
    c = _rms(z[:, Q_LORA:Q_LORA + KV_LORA], kvn_ref[...])
    kr_lo = Q_LORA + KV_LORA
    kr = z[:, kr_lo:kr_lo + LANES] * cos + z[:, kr_lo + LANES:kr_lo + 2 * LANES] * sin
    c_ref[...] = c
    kr_ref[...] = kr[:, :ROPE_DIM]
    kv_ref[:, :KV_LORA] = c.astype(BF16)
    kv_ref[:, KV_LORA:] = kr.astype(BF16)
    qn = _rms(z[:, :Q_LORA], qn_ref[...]).astype(BF16)
    q = _dot(qn, wqb_ref[...])
    rope_lo = MLA_HEADS * NOPE_DIM
    swap_lo = rope_lo + MLA_HEADS * LANES
    for hh in range(MLA_HEADS):
        ql = _dot(q[:, hh * NOPE_DIM:(hh + 1) * NOPE_DIM].astype(BF16), wuk_ref[hh])
        q_ref[:, hh * QK_DIM:hh * QK_DIM + KV_LORA] = (ql * SOFTMAX_SCALE).astype(BF16)
        qr = (q[:, rope_lo + hh * LANES:rope_lo + (hh + 1) * LANES] * cos
              + q[:, swap_lo + hh * LANES:swap_lo + (hh + 1) * LANES] * sin)
        q_ref[:, hh * QK_DIM + KV_LORA:(hh + 1) * QK_DIM] = (qr * SOFTMAX_SCALE).astype(BF16)


def _mla_in(x, cos, sin, g, w_in, q_norm, kv_norm, w_qb, w_uk):
    m = x.shape[0]
    d = x.shape[1]
    return pl.pallas_call(
        _mla_in_kernel,
        grid=(m // ROW_TILE,),
        in_specs=[_rows(d), _rows(LANES), _rows(LANES), _resident(g.shape), _resident(w_in.shape),
                  _resident(q_norm.shape), _resident(kv_norm.shape), _resident(w_qb.shape),
                  _resident(w_uk.shape)],
        out_specs=[_rows(MLA_HEADS * QK_DIM), _rows(QK_DIM), _rows(KV_LORA), _rows(ROPE_DIM)],
        out_shape=[jax.ShapeDtypeStruct((m, MLA_HEADS * QK_DIM), BF16),
                   jax.ShapeDtypeStruct((m, QK_DIM), BF16),
                   jax.ShapeDtypeStruct((m, KV_LORA), F32),
                   jax.ShapeDtypeStruct((m, ROPE_DIM), F32)],
        compiler_params=_params("parallel"),
        name="mla_in",
    )(x, cos, sin, g, w_in, q_norm, kv_norm, w_qb, w_uk)


def _attn_prompt_kernel(q_ref, kt_ref, v_ref, o_ref, q_scr, m_scr, l_scr, acc_scr):
    qi = pl.program_id(1)
    tq = ATTN_Q_TOKENS
    rows = tq * MLA_HEADS
    for hh in range(MLA_HEADS):
        q_scr[hh * tq:(hh + 1) * tq, :] = q_ref[:, hh * QK_DIM:(hh + 1) * QK_DIM]
    m_scr[...] = jnp.full(m_scr.shape, MASK_VALUE, F32)
    l_scr[...] = jnp.zeros(l_scr.shape, F32)
    acc_scr[...] = jnp.zeros(acc_scr.shape, F32)

    def process(first_block, n_blocks, diagonal):
        kt = [kt_ref[first_block + n] for n in range(n_blocks)]
        kt = kt[0] if n_blocks == 1 else jnp.concatenate(kt, axis=-1)
        start = pl.multiple_of(first_block * tq, tq)
        v = v_ref[pl.ds(start, n_blocks * tq), :KV_LORA]
        for r0 in range(0, rows, ATTN_ROW_CHUNK):
            sl = slice(r0, r0 + ATTN_ROW_CHUNK)
            s = _dot(q_scr[sl, :], kt)
            if diagonal:
                t_loc = (r0 + lax.broadcasted_iota(jnp.int32, (ATTN_ROW_CHUNK, 1), 0)) % tq
                col = lax.broadcasted_iota(jnp.int32, (1, n_blocks * tq), 1) - (n_blocks - 1) * tq
                s = jnp.where(col <= t_loc, s, MASK_VALUE)
            tiles = [s[:, n * tq:(n + 1) * tq] for n in range(n_blocks)]
            tile_max = functools.reduce(jnp.maximum, tiles)
            m_prev = m_scr[sl, :]
            m_new = jnp.maximum(m_prev, jnp.max(tile_max, axis=-1, keepdims=True))
            alpha = jnp.exp(m_prev - m_new)
            ps = [jnp.exp(tile - m_new) for tile in tiles]
            l_scr[sl, :] = alpha * l_scr[sl, :] + functools.reduce(jnp.add, ps)
            p = ps[0] if n_blocks == 1 else jnp.concatenate(ps, axis=-1)
            alpha_wide = jnp.concatenate([alpha] * (KV_LORA // tq), axis=-1)
            acc_scr[sl, :] = alpha_wide * acc_scr[sl, :] + _dot(p.astype(BF16), v)
            m_scr[sl, :] = m_new

    def quad(kj, carry):
        process(ATTN_K_GROUP * kj, ATTN_K_GROUP, False)
        return carry

    lax.fori_loop(0, qi // ATTN_K_GROUP, quad, 0)
    rest = qi % ATTN_K_GROUP
    for n in range(ATTN_K_GROUP):
        pl.when(rest == n)(functools.partial(process, qi - n, n + 1, True))
    for hh in range(MLA_HEADS):
        sl = slice(hh * tq, (hh + 1) * tq)
        l_row = jnp.sum(l_scr[sl, :], axis=-1, keepdims=True)
        o_ref[:, hh * KV_LORA:(hh + 1) * KV_LORA] = (acc_scr[sl, :] / l_row).astype(BF16)


def _attn_prompt(q, kt, kv, n_seq):
    blocks = kt.shape[1]
    t = kv.shape[1]
    tq = ATTN_Q_TOKENS
    rows = tq * MLA_HEADS
    return pl.pallas_call(
        _attn_prompt_kernel,
        grid=(n_seq, blocks),
        in_specs=[pl.BlockSpec((tq, MLA_HEADS * QK_DIM), lambda i, j: (i * blocks + j, 0)),
                  pl.BlockSpec((None, blocks, QK_DIM, tq), lambda i, j: (i, 0, 0, 0)),
                  pl.BlockSpec((None, t, QK_DIM), lambda i, j: (i, 0, 0))],
        out_specs=pl.BlockSpec((tq, MLA_HEADS * KV_LORA), lambda i, j: (i * blocks + j, 0)),
        out_shape=jax.ShapeDtypeStruct((q.shape[0], MLA_HEADS * KV_LORA), BF16),
        scratch_shapes=[pltpu.VMEM((rows, QK_DIM), BF16), pltpu.VMEM((rows, tq), F32),
                        pltpu.VMEM((rows, tq), F32), pltpu.VMEM((rows, KV_LORA), F32)],
        compiler_params=_params("parallel", "arbitrary"),
        name="attn_prompt",
    )(q, kt, kv)


def _attn_decode_kernel(pt_ref, q_ref, cn_ref, krn_ref, *rest, n_new):
    lat_pages = rest[:DECODE_PAGES]
    rope_pages = rest[DECODE_PAGES:2 * DECODE_PAGES]
    o_ref, m_scr, l_scr, acc_scr = rest[2 * DECODE_PAGES:]
    g = pl.program_id(1)
    q = q_ref[...]
    ql = q[:, :KV_LORA]
    qr = q[:, KV_LORA:KV_LORA + ROPE_DIM]
    rows = q.shape[0]

    @pl.when(g == 0)
    def _():
        qlf = ql.astype(F32)
        qrf = qr.astype(F32)
        cn = cn_ref[...].astype(BF16).astype(F32)
        krn = krn_ref[...].astype(BF16).astype(F32)
        q_tok = lax.broadcasted_iota(jnp.int32, (rows, 1), 0) % n_new
        cols = []
        for t in range(n_new):
            s_t = (jnp.sum(qlf * cn[t:t + 1, :], axis=-1, keepdims=True)
                   + jnp.sum(qrf * krn[t:t + 1, :], axis=-1, keepdims=True))
            cols.append(jnp.where(q_tok >= t, s_t, MASK_VALUE))
        m0 = cols[0]
        for t in range(1, n_new):
            m0 = jnp.maximum(m0, cols[t])
        l0 = jnp.zeros((rows, 1), F32)
        acc0 = jnp.zeros((rows, KV_LORA), F32)
        for t in range(n_new):
            p_t = jnp.exp(cols[t] - m0)
            l0 = l0 + p_t
            acc0 = acc0 + p_t.astype(BF16).astype(F32) * cn[t:t + 1, :]
        m_scr[...] = m0
        l_scr[...] = l0
        acc_scr[...] = acc0

    per_group = DECODE_PAGES // DECODE_GROUPS
    partials = []
    for grp in range(DECODE_GROUPS):
        idx = range(grp * per_group, (grp + 1) * per_group)
        kbs = [lat_pages[i][...].astype(BF16) for i in idx]
        s = jnp.concatenate(
            [_dot_nt(ql, kb) + _dot(qr, rope_pages[i][...].astype(BF16)) for i, kb in zip(idx, kbs)],
            axis=-1)
        m_g = jnp.max(s, axis=-1, keepdims=True)
        p = jnp.exp(s - m_g)
        l_g = jnp.sum(p, axis=-1, keepdims=True)
        pb = p.astype(BF16)
        acc_g = _dot(pb[:, :PAGE_SIZE], kbs[0])
        for n in range(1, per_group):
            acc_g = acc_g + _dot(pb[:, n * PAGE_SIZE:(n + 1) * PAGE_SIZE], kbs[n])
        partials.append((m_g, l_g, acc_g))

    m_prev = m_scr[...]
    m_new = m_prev
    for m_g, _, _ in partials:
        m_new = jnp.maximum(m_new, m_g)
    alpha = jnp.exp(m_prev - m_new)
    l_new = alpha * l_scr[...]
    acc = alpha * acc_scr[...]
    for m_g, l_g, acc_g in partials:
        w_g = jnp.exp(m_g - m_new)
        l_new = l_new + w_g * l_g
        acc = acc + w_g * acc_g
    l_scr[...] = l_new
    acc_scr[...] = acc
    m_scr[...] = m_new

    @pl.when(g == pl.num_programs(1) - 1)
    def _():
        o_ref[...] = (acc_scr[...] / l_scr[...]).astype(BF16)


def _attn_decode(page_table, q, c_new, kr_new, cache_latent, cache_krope_t):
    db, rows, _ = q.shape
    n_new = c_new.shape[1]
    n_pages = page_table.shape[1]

    def page_spec(shape, i):
        return pl.BlockSpec((None, None) + shape,
                            lambda b, g, pt: (0, pt[b, g * DECODE_PAGES + i], 0, 0))

    grid_spec = pltpu.PrefetchScalarGridSpec(
        num_scalar_prefetch=1,
        grid=(db, n_pages // DECODE_PAGES),
        in_specs=([pl.BlockSpec((None, rows, QK_DIM), lambda b, g, pt: (b, 0, 0)),
                   pl.BlockSpec((None, n_new, KV_LORA), lambda b, g, pt: (b, 0, 0)),
                   pl.BlockSpec((None, n_new, ROPE_DIM), lambda b, g, pt: (b, 0, 0))]
                  + [page_spec((PAGE_SIZE, KV_LORA), i) for i in range(DECODE_PAGES)]
                  + [page_spec((ROPE_DIM, PAGE_SIZE), i) for i in range(DECODE_PAGES)]),
        out_specs=pl.BlockSpec((None, rows, KV_LORA), lambda b, g, pt: (b, 0, 0)),
        scratch_shapes=[pltpu.VMEM((rows, 1), F32), pltpu.VMEM((rows, 1), F32),
                        pltpu.VMEM((rows, KV_LORA), F32)],
    )
    return pl.pallas_call(
        functools.partial(_attn_decode_kernel, n_new=n_new),
        grid_spec=grid_spec,
        out_shape=jax.ShapeDtypeStruct((db, rows, KV_LORA), BF16),
        compiler_params=_params("parallel", "arbitrary"),
        name="attn_decode",
    )(page_table, q, c_new, kr_new,
      *([cache_latent] * DECODE_PAGES), *([cache_krope_t] * DECODE_PAGES))


def _mla_out_kernel(o_ref, x_ref, wuv_ref, wo_ref, out_ref):
    us = [_dot(o_ref[:, hh * KV_LORA:(hh + 1) * KV_LORA], wuv_ref[hh]).astype(BF16)
          for hh in range(MLA_HEADS)]
    out_ref[...] = x_ref[...] + _dot(jnp.concatenate(us, axis=-1), wo_ref[...])


def _mla_out(o, x, w_uv, w_o):
    m, d = x.shape
    return pl.pallas_call(
        _mla_out_kernel,
        grid=(m // ROW_TILE,),
        in_specs=[_rows(o.shape[1]), _rows(d), _resident(w_uv.shape), _resident(w_o.shape)],
        out_specs=_rows(d),
        out_shape=jax.ShapeDtypeStruct((m, d), F32),
        compiler_params=_params("parallel"),
        name="mla_out",
    )(o, x, w_uv, w_o)


def _ffn_kernel(x_ref, g_ref, wgu_ref, wd_ref, gn_ref, x_out_ref, hn_ref):
    x = x_ref[...]
    h = _rms(x, g_ref[...]).astype(BF16)
    acc = x
    for ci in range(wgu_ref.shape[0]):
        gu = _dot(h, wgu_ref[ci])
        gate = gu[:, :FFN_CHUNK]
        act = (gate * _sigmoid(gate) * gu[:, FFN_CHUNK:]).astype(BF16)
        acc = acc + _dot(act, wd_ref[ci])
    x_out_ref[...] = acc
    hn_ref[...] = _rms(acc, gn_ref[...])


def _ffn(x, g, wgu, wd, g_next):
    m, d = x.shape
    return pl.pallas_call(
        _ffn_kernel,
        grid=(m // ROW_TILE,),
        in_specs=[_rows(d), _resident(g.shape), _resident(wgu.shape), _resident(wd.shape),
                  _resident(g_next.shape)],
        out_specs=[_rows(d), _rows(d)],
        out_shape=[jax.ShapeDtypeStruct((m, d), F32), jax.ShapeDtypeStruct((m, d), F32)],
        compiler_params=_params("parallel"),
        name="ffn",
    )(x, g, wgu, wd, g_next)


def _time_mix_kernel(h_ref, xp_ref, mix_ref, wr_ref, wk_ref, wv_ref, w0_ref, w1_ref, w2_ref,
                     a0_ref, a1_ref, a2_ref, g1_ref, g2_ref,
                     r_ref, w_ref, k_ref, v_ref, a_ref, g_ref):
    h = h_ref[...]
    xx = xp_ref[...] - h
    mix = mix_ref[...]

    def mixed(i):
        return (h + xx * mix[i:i + 1, :]).astype(BF16)

    def store_pairs(ref, val):
        for p in range(ref.shape[0]):
            ref[p] = val[:, p * LANES:(p + 1) * LANES]

    store_pairs(r_ref, _dot(mixed(0), wr_ref[...]))
    lw = jnp.tanh(_dot(mixed(1), w1_ref[...])).astype(BF16)
    z = -(w0_ref[...] + _dot(lw, w2_ref[...]))
    softplus = jnp.maximum(z, 0.0) + jnp.log(1.0 + jnp.exp(-jnp.abs(z)))
    store_pairs(w_ref, jnp.exp(-jnp.exp(-softplus - 0.5)))
    store_pairs(k_ref, _dot(mixed(2), wk_ref[...]))
    store_pairs(v_ref, _dot(mixed(3), wv_ref[...]))
    al = _dot(mixed(4), a1_ref[...]).astype(BF16)
    store_pairs(a_ref, _sigmoid(a0_ref[...] + _dot(al, a2_ref[...])))
    gl = _sigmoid(_dot(mixed(5), g1_ref[...])).astype(BF16)
    g_ref[...] = _dot(gl, g2_ref[...])


def _pair_rows(pairs, tile=ROW_TILE):
    return pl.BlockSpec((pairs, tile, LANES), lambda i: (0, i, 0))


def _time_mix(h, xprev, mix, w_r, w_k, w_v, w0, w1, w2, a0, a1, a2, g1, g2):
    m, d = h.shape
    pairs = d // LANES
    weights = (mix, w_r, w_k, w_v, w0, w1, w2, a0, a1, a2, g1, g2)
    return pl.pallas_call(
        _time_mix_kernel,
        grid=(m // ROW_TILE,),
        in_specs=[_rows(d), _rows(d)] + [_resident(w.shape) for w in weights],
        out_specs=[_pair_rows(pairs)] * 5 + [_rows(d)],
        out_shape=[jax.ShapeDtypeStruct((pairs, m, LANES), F32)] * 5 + [jax.ShapeDtypeStruct((m, d), F32)],
        compiler_params=_params("parallel"),
        name="time_mix",
    )(h, xprev, *weights)


def _wkv_kernel(r_ref, w_ref, k_ref, v_ref, a_ref, s0_ref, kk_ref, ka_ref, rk_ref, lnw_ref, lnb_ref,
                y_ref, st_ref, s_scr, v_scr, y_scr, p_scr, *, rows_per_pair, steps, real_blocks):
    step_block = pl.program_id(1)
    n = s_scr.shape[1]
    pairs = r_ref.shape[0]

    @pl.when(step_block == 0)
    def _():
        s_scr[...] = s0_ref[...]

    def channels_by_lane(ref, off):
        tile = [ref[p, pl.ds(off, rows_per_pair), :] for p in range(pairs)]
        tile = tile[0] if pairs == 1 else jnp.concatenate(tile, axis=0)
        return tile.T

    def step(t, carry):
        off = pl.multiple_of(t * rows_per_pair, rows_per_pair)
        r2 = channels_by_lane(r_ref, off)
        w2 = channels_by_lane(w_ref, off)
        k2 = channels_by_lane(k_ref, off)
        a2 = channels_by_lane(a_ref, off)
        v_scr[...] = channels_by_lane(v_ref, off)
        for half in range(2):
            sl = slice(half * n, (half + 1) * n)
            r, w, k, a = r2[sl], w2[sl], k2[sl], a2[sl]
            kk = k * kk_ref[half]
            kk = kk / jnp.maximum(jnp.sqrt(jnp.sum(kk * kk, axis=0, keepdims=True)), 1e-12)
            b = kk * a
            k = k * (1.0 + (a - 1.0) * ka_ref[half])
            prod_prev = p_scr[half]
            prod = prod_prev * w
            p_scr[half] = prod
            inv = 1.0 / prod
            kk_s = kk * prod_prev
            b_s = b * inv
            k_s = k * inv
            r_s = r * prod
            for i in range(n):
                s_i = s_scr[half, i]
                sa = -jnp.sum(s_i * kk_s, axis=0, keepdims=True)
                v_i = v_scr[half * n + i:half * n + i + 1, :]
                s_i = s_i + sa * b_s + v_i * k_s
                s_scr[half, i] = s_i
                y_scr[half * n + i:half * n + i + 1, :] = jnp.sum(s_i * r_s, axis=0, keepdims=True)
            y = y_scr[sl, :]
            mu = jnp.mean(y, axis=0, keepdims=True)
            yc = y - mu
            var = jnp.mean(yc * yc, axis=0, keepdims=True)
            bonus = jnp.sum(r * k * rk_ref[half], axis=0, keepdims=True) * v_scr[sl, :]
            y_scr[sl, :] = yc * lax.rsqrt(var + LNX_EPS) * lnw_ref[half] + lnb_ref[half] + bonus
        y_rows = y_scr[...].T
        for p in range(pairs):
            y_ref[p, pl.ds(off, rows_per_pair), :] = y_rows[p * rows_per_pair:(p + 1) * rows_per_pair, :]
        return carry

    @pl.when(step_block < real_blocks)
    def _():
        p_scr[...] = jnp.ones(p_scr.shape, F32)
        lax.fori_loop(0, steps, step, 0)
        for half in range(2):
            prod = p_scr[half]

            def rescale(i, c2, half=half, prod=prod):
                s_scr[half, i] = s_scr[half, i] * prod
                return c2

            lax.fori_loop(0, n, rescale, 0, unroll=8)

    @pl.when(step_block >= real_blocks)
    def _():
        y_ref[...] = jnp.zeros(y_ref.shape, F32)

    @pl.when(step_block == real_blocks - 1)
    def _():
        st_ref[...] = s_scr[...]


def _wkv(r, w, k, v, a, s0, k_k, k_a, r_k, lnw, lnb, batch, real_steps, steps_per_block):
    pairs, rows, _ = r.shape
    n = RWKV_HEAD
    group = WKV_LANES // batch
    block_rows = steps_per_block * batch
    kern = functools.partial(_wkv_kernel, rows_per_pair=batch, steps=steps_per_block,
                             real_blocks=real_steps // steps_per_block)
    seq = pl.BlockSpec((group, block_rows, LANES), lambda i, j: (i, j, 0))
    state = pl.BlockSpec((None, 2, n, n, WKV_LANES), lambda i, j: (i, 0, 0, 0, 0))
    table = pl.BlockSpec((None, 2, n, WKV_LANES), lambda i, j: (i, 0, 0, 0))
    return pl.pallas_call(
        kern,
        grid=(pairs // group, rows // block_rows),
        in_specs=[seq] * 5 + [state] + [table] * 5,
        out_specs=[seq, state],
        out_shape=[jax.ShapeDtypeStruct(r.shape, F32), jax.ShapeDtypeStruct(s0.shape, F32)],
        scratch_shapes=[pltpu.VMEM((2, n, n, WKV_LANES), F32), pltpu.VMEM((2 * n, WKV_LANES), F32),
                        pltpu.VMEM((2 * n, WKV_LANES), F32), pltpu.VMEM((2, n, WKV_LANES), F32)],
        compiler_params=_params("parallel", "arbitrary"),
        name="wkv",
    )(r, w, k, v, a, s0, k_k, k_a, r_k, lnw, lnb)


def _rwkv_out_kernel(y_ref, g_ref, x_ref, wo_ref, out_ref):
    y = jnp.concatenate([y_ref[p] for p in range(y_ref.shape[0])], axis=-1)
    out_ref[...] = x_ref[...] + _dot((y * g_ref[...]).astype(BF16), wo_ref[...])


def _rwkv_out(y, g, x, w_o):
    m, d = x.shape
    return pl.pallas_call(
        _rwkv_out_kernel,
        grid=(m // ROW_TILE,),
        in_specs=[_pair_rows(y.shape[0]), _rows(d), _rows(d), _resident(w_o.shape)],
        out_specs=_rows(d),
        out_shape=jax.ShapeDtypeStruct((m, d), F32),
        compiler_params=_params("parallel"),
        name="rwkv_out",
    )(y, g, x, w_o)


def _rope_tables(pos, reps):
    half = ROPE_DIM // 2
    inv = ROPE_THETA ** (-jnp.arange(half, dtype=F32) / half)
    ang = pos.astype(F32)[:, None] * inv[None, :]
    cos, sin = jnp.cos(ang), jnp.sin(ang)
    cos = jnp.tile(jnp.concatenate([cos, cos], axis=-1), (reps, LANES // ROPE_DIM))
    sin = jnp.tile(jnp.concatenate([-sin, sin], axis=-1), (reps, LANES // ROPE_DIM))
    return cos, sin


def _swap_halves(w):
    half = w.shape[-1] // 2
    return jnp.concatenate([w[..., half:], w[..., :half]], axis=-1)


def _pad_lanes(w):
    return jnp.pad(w, [(0, 0)] * (w.ndim - 1) + [(0, LANES - w.shape[-1])])


def _time_major(x, batch):
    d = x.shape[-1]
    return x.reshape(batch, -1, d).transpose(1, 0, 2).reshape(-1, d)


def _pair_table(p, batch):
    pairs = p.shape[0] // LANES
    group = WKV_LANES // batch
    t = p.reshape(pairs // group, group, 2, RWKV_HEAD).transpose(0, 2, 3, 1)
    return jnp.repeat(t, batch, axis=-1)


def _state_to_lanes(s, batch):
    heads, n = s.shape[1], s.shape[2]
    group = WKV_LANES // batch
    s = s.reshape(batch, heads // (2 * group), group, 2, n, n)
    return s.transpose(1, 3, 4, 5, 2, 0).reshape(heads // (2 * group), 2, n, n, WKV_LANES)


def _state_from_lanes(s, batch):
    groups, _, n, _, _ = s.shape
    group = WKV_LANES // batch
    s = s.reshape(groups, 2, n, n, group, batch).transpose(5, 0, 4, 1, 2, 3)
    return s.reshape(batch, groups * group * 2, n, n)


def kernel(x_prompt, x_sample, cache_latent, cache_krope, state_shift, state_wkv, page_table, meta_tokens, ln_mix, ln_ffn, ln_final, attn_w_in, attn_q_norm, attn_kv_norm, attn_w_qb, attn_w_uk, attn_w_uv, attn_w_o, rwkv_mix, rwkv_w_r, rwkv_w_k, rwkv_w_v, rwkv_w_o, rwkv_w0, rwkv_w1, rwkv_w2, rwkv_a0, rwkv_a1, rwkv_a2, rwkv_g1, rwkv_g2, rwkv_k_k, rwkv_k_a, rwkv_r_k, rwkv_lnx_w, rwkv_lnx_b, ffn_w_gate, ffn_w_up, ffn_w_down):
    nb, seq, d = x_prompt.shape
    db, ds, _ = x_sample.shape
    t_real = seq + N_META
    t_pad = -(-t_real // SEQ_ALIGN) * SEQ_ALIGN
    past_len = page_table.shape[1] * PAGE_SIZE
    heads = d // RWKV_HEAD
    row = lambda p: p.reshape(1, -1).astype(F32)

    meta = jnp.broadcast_to(meta_tokens[None].astype(x_prompt.dtype), (nb, N_META, d))
    xp = jnp.concatenate([meta, x_prompt, jnp.zeros((nb, t_pad - t_real, d), x_prompt.dtype)], axis=1)
    xp = xp.reshape(nb * t_pad, d)
    xs = x_sample.reshape(db * ds, d)
    cos_p, sin_p = _rope_tables(jnp.arange(t_pad), nb)
    cos_s, sin_s = _rope_tables(past_len + jnp.arange(ds), db)

    w_in = attn_w_in[0]
    kr_cols = w_in[:, Q_LORA + KV_LORA:]
    w_in_ext = jnp.concatenate(
        [w_in[:, :Q_LORA + KV_LORA], _pad_lanes(kr_cols), _pad_lanes(_swap_halves(kr_cols))],
        axis=-1).astype(BF16)
    w_qb = attn_w_qb[0].reshape(Q_LORA, MLA_HEADS, NOPE_DIM + ROPE_DIM)
    q_rope_cols = w_qb[:, :, NOPE_DIM:]
    w_qb_ext = jnp.concatenate(
        [w_qb[:, :, :NOPE_DIM].reshape(Q_LORA, -1), _pad_lanes(q_rope_cols).reshape(Q_LORA, -1),
         _pad_lanes(_swap_halves(q_rope_cols)).reshape(Q_LORA, -1)], axis=-1).astype(BF16)
    w_uk_t = attn_w_uk[0].transpose(1, 2, 0).astype(BF16)
    w_uv = attn_w_uv[0].transpose(1, 0, 2).astype(BF16)
    w_o = attn_w_o[0].astype(BF16)
    mla_w = (row(ln_mix[0]), w_in_ext, row(attn_q_norm[0]), row(attn_kv_norm[0]), w_qb_ext, w_uk_t)

    q_p, kv_p, c_p, kr_p = _mla_in(xp, cos_p, sin_p, *mla_w)
    q_s, _, c_s, kr_s = _mla_in(xs, cos_s, sin_s, *mla_w)

    kv3 = kv_p.reshape(nb, t_pad, QK_DIM)
    kt4 = kv3.reshape(nb, t_pad // ATTN_Q_TOKENS, ATTN_Q_TOKENS, QK_DIM).transpose(0, 1, 3, 2)
    o_p = _attn_prompt(q_p, kt4, kv3, nb)
    q_s3 = q_s.reshape(db, ds, MLA_HEADS, QK_DIM).transpose(0, 2, 1, 3).reshape(db, MLA_HEADS * ds, QK_DIM)
    o_s = _attn_decode(page_table, q_s3, c_s.reshape(db, ds, KV_LORA), kr_s.reshape(db, ds, ROPE_DIM),
                       cache_latent[:1], jnp.swapaxes(cache_krope[:1], 2, 3))
    o_s = o_s.reshape(db, MLA_HEADS, ds, KV_LORA).transpose(0, 2, 1, 3).reshape(db * ds, MLA_HEADS * KV_LORA)

    def ffn_weights(i):
        n_chunks = ffn_w_gate.shape[2] // FFN_CHUNK
        wg = ffn_w_gate[i].reshape(d, n_chunks, FFN_CHUNK)
        wu = ffn_w_up[i].reshape(d, n_chunks, FFN_CHUNK)
        wgu = jnp.concatenate([wg, wu], axis=-1).transpose(1, 0, 2).astype(BF16)
        wd = ffn_w_down[i].reshape(n_chunks, FFN_CHUNK, d).astype(BF16)
        return wgu, wd

    ffn0 = (row(ln_ffn[0]),) + ffn_weights(0) + (row(ln_mix[1]),)
    xp, hp = _ffn(_mla_out(o_p, xp, w_uv, w_o), *ffn0)
    xs, hs = _ffn(_mla_out(o_s, xs, w_uv, w_o), *ffn0)

    xp, hp, xs, hs = _time_major(xp, nb), _time_major(hp, nb), _time_major(xs, db), _time_major(hs, db)
    prev_p = jnp.concatenate([jnp.zeros((nb, d), F32), hp[:-nb]], axis=0)
    prev_s = jnp.concatenate([state_shift[0].astype(F32), hs[:-db]], axis=0)

    def lora(w_down, w_up):
        rank = w_down.shape[1]
        pad = -(-rank // LANES) * LANES - rank
        return (jnp.pad(w_down, ((0, 0), (0, pad))).astype(BF16),
                jnp.pad(w_up, ((0, pad), (0, 0))).astype(BF16))

    w1, w2 = lora(rwkv_w1[0], rwkv_w2[0])
    a1, a2 = lora(rwkv_a1[0], rwkv_a2[0])
    g1, g2 = lora(rwkv_g1[0], rwkv_g2[0])
    tm_w = (rwkv_mix[0].astype(F32), rwkv_w_r[0].astype(BF16), rwkv_w_k[0].astype(BF16),
            rwkv_w_v[0].astype(BF16), row(rwkv_w0[0]), w1, w2, row(rwkv_a0[0]), a1, a2, g1, g2)
    tables = (rwkv_k_k[0], rwkv_k_a[0], rwkv_r_k[0].reshape(-1), rwkv_lnx_w[0], rwkv_lnx_b[0])
    w_o1 = rwkv_w_o[0].astype(BF16)

    def time_mixing(h, prev, x, batch, steps, s0, steps_per_block):
        r, w, k, v, a, g = _time_mix(h, prev, *tm_w)
        tabs = [_pair_table(p.astype(F32), batch) for p in tables]
        y, s_t = _wkv(r, w, k, v, a, _state_to_lanes(s0, batch), *tabs, batch, steps, steps_per_block)
        return _rwkv_out(y, g, x, w_o1), _state_from_lanes(s_t, batch)

    s0_p = jnp.zeros((nb, heads, RWKV_HEAD, RWKV_HEAD), F32)
    xp, wkv_p = time_mixing(hp, prev_p, xp, nb, t_real, s0_p, WKV_STEPS)
    xs, wkv_s = time_mixing(hs, prev_s, xs, db, ds, state_wkv[0].astype(F32), ds)

    ffn1 = (row(ln_ffn[1]),) + ffn_weights(1) + (row(ln_final),)
    _, yp = _ffn(xp, *ffn1)
    _, ys = _ffn(xs, *ffn1)

    seq3 = lambda u: u.reshape(nb, t_pad, -1)[:, :t_real]
    return (yp.reshape(t_pad, nb, d)[N_META:t_real].transpose(1, 0, 2),
            ys.reshape(ds, db, d).transpose(1, 0, 2),
            seq3(c_p)[None],
            seq3(kr_p)[None],
            hp[(t_real - 1) * nb:t_real * nb][None],
            wkv_p.astype(state_wkv.dtype)[None],
            c_s.reshape(db, ds, KV_LORA)[None],
            kr_s.reshape(db, ds, ROPE_DIM)[None],
            hs[(ds - 1) * db:][None],
            wkv_s.astype(state_wkv.dtype)[None])
```

```python
import functools

import jax
import jax.numpy as jnp
from jax import lax
from jax.experimental import pallas as pl
from jax.experimental.pallas import tpu as pltpu

F32 = jnp.float32
BF16 = jnp.bfloat16

N_META = 16
RMS_EPS = 1e-6
MLA_HEADS = 8
Q_LORA = 512
KV_LORA = 256
NOPE_DIM = 128
ROPE_DIM = 64
V_DIM = 128
ROPE_THETA = 10000.0
SOFTMAX_SCALE = (NOPE_DIM + ROPE_DIM) ** -0.5
PAGE_SIZE = 128
RWKV_HEAD = 64
LNX_EPS = 64e-5
MASK_VALUE = -1e30

LANES = 128
QK_DIM = KV_LORA + LANES
ROW_TILE = 512
SEQ_ALIGN = 128
ATTN_Q_TOKENS = 128
ATTN_ROW_CHUNK = 512
ATTN_K_GROUP = 8
DECODE_PAGES = 32
DECODE_GROUPS = 2
FFN_CHUNK = 256
WKV_LANES = 128
WKV_STEPS = 16
VMEM_LIMIT = 56 * 1024 * 1024


def _params(*sem):
    return pltpu.CompilerParams(dimension_semantics=sem, vmem_limit_bytes=VMEM_LIMIT)


def _resident(shape):
    nd = len(shape)
    return pl.BlockSpec(shape, lambda *_: (0,) * nd, pipeline_mode=pl.Buffered(1))


def _rows(width, tile=ROW_TILE):
    return pl.BlockSpec((tile, width), lambda i: (i, 0))


def _rms(x, g):
    return x * lax.rsqrt(jnp.mean(x * x, axis=-1, keepdims=True) + RMS_EPS) * g


def _sigmoid(x):
    return 1.0 / (1.0 + jnp.exp(-x))


def _dot(a, b):
    return jnp.dot(a, b, preferred_element_type=F32)


def _dot_nt(a, b):
    return lax.dot_general(a, b, (((1,), (1,)), ((), ())), preferred_element_type=F32)


def _mla_in_kernel(x_ref, cos_ref, sin_ref, g_ref, win_ref, qn_ref, kvn_ref, wqb_ref, wuk_ref,
                   q_ref, kv_ref, c_ref, kr_ref):
    h = _rms(x_ref[...], g_ref[...]).astype(BF16)
    z = _dot(h, win_ref[...])
    cos = cos_ref[...]
    sin = sin_ref[...]
    c = _rms(z[:, Q_LORA:Q_LORA + KV_LORA], kvn_ref[...])
    kr_lo = Q_LORA + KV_LORA
    kr = z[:, kr_lo:kr_lo + LANES] * cos + z[:, kr_lo + LANES:kr_lo + 2 * LANES] * sin
    c_ref[...] = c
    kr_ref[...] = kr[:, :ROPE_DIM]
    kv_ref[:, :KV_LORA] = c.astype(BF16)
    kv_ref[:, KV_LORA:] = kr.astype(BF16)
    qn = _rms(z[:, :Q_LORA], qn_ref[...]).astype(BF16)
    q = _dot(qn, wqb_ref[...])
    rope_lo = MLA_HEADS * NOPE_DIM
    swap_lo = rope_lo + MLA_HEADS * LANES
    for hh in range(MLA_HEADS):
        ql = _dot(q[:, hh * NOPE_DIM:(hh + 1) * NOPE_DIM].astype(BF16), wuk_ref[hh])
        q_ref[:, hh * QK_DIM:hh * QK_DIM + KV_LORA] = (ql * SOFTMAX_SCALE).astype(BF16)
        qr = (q[:, rope_lo + hh * LANES:rope_lo + (hh + 1) * LANES] * cos
              + q[:, swap_lo + hh * LANES:swap_lo + (hh + 1) * LANES] * sin)
        q_ref[:, hh * QK_DIM + KV_LORA:(hh + 1) * QK_DIM] = (qr * SOFTMAX_SCALE).astype(BF16)


def _mla_in(x, cos, sin, g, w_in, q_norm, kv_norm, w_qb, w_uk):
    m = x.shape[0]
    d = x.shape[1]
    return pl.pallas_call(
        _mla_in_kernel,
        grid=(m // ROW_TILE,),
        in_specs=[_rows(d), _rows(LANES), _rows(LANES), _resident(g.shape), _resident(w_in.shape),
                  _resident(q_norm.shape), _resident(kv_norm.shape), _resident(w_qb.shape),
                  _resident(w_uk.shape)],
        out_specs=[_rows(MLA_HEADS * QK_DIM), _rows(QK_DIM), _rows(KV_LORA), _rows(ROPE_DIM)],
        out_shape=[jax.ShapeDtypeStruct((m, MLA_HEADS * QK_DIM), BF16),
                   jax.ShapeDtypeStruct((m, QK_DIM), BF16),
                   jax.ShapeDtypeStruct((m, KV_LORA), F32),
                   jax.ShapeDtypeStruct((m, ROPE_DIM), F32)],
        compiler_params=_params("parallel"),
        name="mla_in",
    )(x, cos, sin, g, w_in, q_norm, kv_norm, w_qb, w_uk)


def _attn_prompt_kernel(q_ref, kt_ref, v_ref, o_ref, q_scr, m_scr, l_scr, acc_scr):
    qi = pl.program_id(1)
    tq = ATTN_Q_TOKENS
    rows = tq * MLA_HEADS
    for hh in range(MLA_HEADS):
        q_scr[hh * tq:(hh + 1) * tq, :] = q_ref[:, hh * QK_DIM:(hh + 1) * QK_DIM]
    m_scr[...] = jnp.full(m_scr.shape, MASK_VALUE, F32)
    l_scr[...] = jnp.zeros(l_scr.shape, F32)
    acc_scr[...] = jnp.zeros(acc_scr.shape, F32)

    def process(first_block, n_blocks, diagonal):
        kt = [kt_ref[first_block + n] for n in range(n_blocks)]
        kt = kt[0] if n_blocks == 1 else jnp.concatenate(kt, axis=-1)
        start = pl.multiple_of(first_block * tq, tq)
        v = v_ref[pl.ds(start, n_blocks * tq), :KV_LORA]
        for r0 in range(0, rows, ATTN_ROW_CHUNK):
            sl = slice(r0, r0 + ATTN_ROW_CHUNK)
            s = _dot(q_scr[sl, :], kt)
            if diagonal:
                t_loc = (r0 + lax.broadcasted_iota(jnp.int32, (ATTN_ROW_CHUNK, 1), 0)) % tq
                col = lax.broadcasted_iota(jnp.int32, (1, n_blocks * tq), 1) - (n_blocks - 1) * tq
                s = jnp.where(col <= t_loc, s, MASK_VALUE)
            tiles = [s[:, n * tq:(n + 1) * tq] for n in range(n_blocks)]
            tile_max = functools.reduce(jnp.maximum, tiles)
            m_prev = m_scr[sl, :]
            m_new = jnp.maximum(m_prev, jnp.max(tile_max, axis=-1, keepdims=True))
            alpha = jnp.exp(m_prev - m_new)
            ps = [jnp.exp(tile - m_new) for tile in tiles]
            l_scr[sl, :] = alpha * l_scr[sl, :] + functools.reduce(jnp.add, ps)
            p = ps[0] if n_blocks == 1 else jnp.concatenate(ps, axis=-1)
            alpha_wide = jnp.concatenate([alpha] * (KV_LORA // tq), axis=-1)
            acc_scr[sl, :] = alpha_wide * acc_scr[sl, :] + _dot(p.astype(BF16), v)
            m_scr[sl, :] = m_new

    def quad(kj, carry):
        process(ATTN_K_GROUP * kj, ATTN_K_GROUP, False)
        return carry

    lax.fori_loop(0, qi // ATTN_K_GROUP, quad, 0)
    rest = qi % ATTN_K_GROUP
    for n in range(ATTN_K_GROUP):
        pl.when(rest == n)(functools.partial(process, qi - n, n + 1, True))
    for hh in range(MLA_HEADS):
        sl = slice(hh * tq, (hh + 1) * tq)
        l_row = jnp.sum(l_scr[sl, :], axis=-1, keepdims=True)
        o_ref[:, hh * KV_LORA:(hh + 1) * KV_LORA] = (acc_scr[sl, :] / l_row).astype(BF16)


def _attn_prompt(q, kt, kv, n_seq):
    blocks = kt.shape[1]
    t = kv.shape[1]
    tq = ATTN_Q_TOKENS
    rows = tq * MLA_HEADS
    return pl.pallas_call(
        _attn_prompt_kernel,
        grid=(n_seq, blocks),
        in_specs=[pl.BlockSpec((tq, MLA_HEADS * QK_DIM), lambda i, j: (i * blocks + j, 0)),
                  pl.BlockSpec((None, blocks, QK_DIM, tq), lambda i, j: (i, 0, 0, 0)),
                  pl.BlockSpec((None, t, QK_DIM), lambda i, j: (i, 0, 0))],
        out_specs=pl.BlockSpec((tq, MLA_HEADS * KV_LORA), lambda i, j: (i * blocks + j, 0)),
        out_shape=jax.ShapeDtypeStruct((q.shape[0], MLA_HEADS * KV_LORA), BF16),
        scratch_shapes=[pltpu.VMEM((rows, QK_DIM), BF16), pltpu.VMEM((rows, tq), F32),
                        pltpu.VMEM((rows, tq), F32), pltpu.VMEM((rows, KV_LORA), F32)],
        compiler_params=_params("parallel", "arbitrary"),
        name="attn_prompt",
    )(q, kt, kv)


def _attn_decode_kernel(pt_ref, q_ref, cn_ref, krn_ref, lat_hbm, rope_hbm, o_ref,
                        lat_buf, rope_buf, sem, *, n_new, n_groups):
    b = pl.program_id(0)
    q = q_ref[...]
    ql = q[:, :KV_LORA]
    qr = q[:, KV_LORA:KV_LORA + ROPE_DIM]
    rows = q.shape[0]

    def page_copies(page, slot, i):
        return (pltpu.make_async_copy(lat_hbm.at[0, page], lat_buf.at[slot, i], sem.at[slot, 0]),
                pltpu.make_async_copy(rope_hbm.at[0, page], rope_buf.at[slot, i], sem.at[slot, 1]))

    def start_group(row, group, slot):
        for i in range(DECODE_PAGES):
            for cp in page_copies(pt_ref[row, group * DECODE_PAGES + i], slot, i):
                cp.start()

    def wait_group(slot):
        for i in range(DECODE_PAGES):
            for cp in page_copies(0, slot, i):
                cp.wait()

    @pl.when(b == 0)
    def _():
        start_group(0, 0, 0)

    qlf = ql.astype(F32)
    qrf = qr.astype(F32)
    cn = cn_ref[...].astype(BF16).astype(F32)
    krn = krn_ref[...].astype(BF16).astype(F32)
    q_tok = lax.broadcasted_iota(jnp.int32, (rows, 1), 0) % n_new
    cols = []
    for t in range(n_new):
        s_t = (jnp.sum(qlf * cn[t:t + 1, :], axis=-1, keepdims=True)
               + jnp.sum(qrf * krn[t:t + 1, :], axis=-1, keepdims=True))
        cols.append(jnp.where(q_tok >= t, s_t, MASK_VALUE))
    m_run = functools.reduce(jnp.maximum, cols)
    l_run = jnp.zeros((rows, 1), F32)
    acc = jnp.zeros((rows, KV_LORA), F32)
    for t in range(n_new):
        p_t = jnp.exp(cols[t] - m_run)
        l_run = l_run + p_t
        acc = acc + p_t.astype(BF16).astype(F32) * cn[t:t + 1, :]

    per_part = DECODE_PAGES // DECODE_GROUPS
    for group in range(n_groups):
        slot = group % 2
        if group + 1 < n_groups:
            start_group(b, group + 1, 1 - slot)
        else:
            pl.when(b + 1 < pl.num_programs(0))(functools.partial(start_group, b + 1, 0, 1 - slot))
        wait_group(slot)
        partials = []
        for part in range(DECODE_GROUPS):
            idx = range(part * per_part, (part + 1) * per_part)
            kbs = [lat_buf[slot, i].astype(BF16) for i in idx]
            s = jnp.concatenate(
                [_dot_nt(ql, kb) + _dot(qr, rope_buf[slot, i].astype(BF16)) for i, kb in zip(idx, kbs)],
                axis=-1)
            m_g = jnp.max(s, axis=-1, keepdims=True)
            p = jnp.exp(s - m_g)
            l_g = jnp.sum(p, axis=-1, keepdims=True)
            pb = p.astype(BF16)
            acc_g = _dot(pb[:, :PAGE_SIZE], kbs[0])
            for n in range(1, per_part):
                acc_g = acc_g + _dot(pb[:, n * PAGE_SIZE:(n + 1) * PAGE_SIZE], kbs[n])
            partials.append((m_g, l_g, acc_g))
        m_new = functools.reduce(jnp.maximum, [m_g for m_g, _, _ in partials], m_run)
        alpha = jnp.exp(m_run - m_new)
        l_run = alpha * l_run
        acc = alpha * acc
        for m_g, l_g, acc_g in partials:
            w_g = jnp.exp(m_g - m_new)
            l_run = l_run + w_g * l_g
            acc = acc + w_g * acc_g
        m_run = m_new
    o_ref[...] = (acc / l_run).astype(BF16)


def _attn_decode(page_table, q, c_new, kr_new, cache_latent, cache_krope_t):
    db, rows, _ = q.shape
    n_new = c_new.shape[1]
    n_groups = page_table.shape[1] // DECODE_PAGES
    assert n_groups % 2 == 0, "the prefetch of the next row's first group assumes it lands in slot 0"
    grid_spec = pltpu.PrefetchScalarGridSpec(
        num_scalar_prefetch=1,
        grid=(db,),
        in_specs=[pl.BlockSpec((None, rows, QK_DIM), lambda b, pt: (b, 0, 0)),
                  pl.BlockSpec((None, n_new, KV_LORA), lambda b, pt: (b, 0, 0)),
                  pl.BlockSpec((None, n_new, ROPE_DIM), lambda b, pt: (b, 0, 0)),
                  pl.BlockSpec(memory_space=pl.ANY),
                  pl.BlockSpec(memory_space=pl.ANY)],
        out_specs=pl.BlockSpec((None, rows, KV_LORA), lambda b, pt: (b, 0, 0)),
        scratch_shapes=[pltpu.VMEM((2, DECODE_PAGES, PAGE_SIZE, KV_LORA), F32),
                        pltpu.VMEM((2, DECODE_PAGES, ROPE_DIM, PAGE_SIZE), F32),
                        pltpu.SemaphoreType.DMA((2, 2))],
    )
    return pl.pallas_call(
        functools.partial(_attn_decode_kernel, n_new=n_new, n_groups=n_groups),
        grid_spec=grid_spec,
        out_shape=jax.ShapeDtypeStruct((db, rows, KV_LORA), BF16),
        compiler_params=_params("arbitrary"),
        name="attn_decode",
    )(page_table, q, c_new, kr_new, cache_latent, cache_krope_t)


def _mla_out_kernel(o_ref, x_ref, wuv_ref, wo_ref, out_ref):
    us = [_dot(o_ref[:, hh * KV_LORA:(hh + 1) * KV_LORA], wuv_ref[hh]).astype(BF16)
          for hh in range(MLA_HEADS)]
    out_ref[...] = x_ref[...] + _dot(jnp.concatenate(us, axis=-1), wo_ref[...])


def _mla_out(o, x, w_uv, w_o):
    m, d = x.shape
    return pl.pallas_call(
        _mla_out_kernel,
        grid=(m // ROW_TILE,),
        in_specs=[_rows(o.shape[1]), _rows(d), _resident(w_uv.shape), _resident(w_o.shape)],
        out_specs=_rows(d),
        out_shape=jax.ShapeDtypeStruct((m, d), F32),
        compiler_params=_params("parallel"),
        name="mla_out",
    )(o, x, w_uv, w_o)


def _ffn_kernel(x_ref, g_ref, wgu_ref, wd_ref, gn_ref, x_out_ref, hn_ref):
    x = x_ref[...]
    h = _rms(x, g_ref[...]).astype(BF16)
    acc = x
    for ci in range(wgu_ref.shape[0]):
        gu = _dot(h, wgu_ref[ci])
        gate = gu[:, :FFN_CHUNK]
        act = (gate * _sigmoid(gate) * gu[:, FFN_CHUNK:]).astype(BF16)
        acc = acc + _dot(act, wd_ref[ci])
    x_out_ref[...] = acc
    hn_ref[...] = _rms(acc, gn_ref[...])


def _ffn(x, g, wgu, wd, g_next):
    m, d = x.shape
    return pl.pallas_call(
        _ffn_kernel,
        grid=(m // ROW_TILE,),
        in_specs=[_rows(d), _resident(g.shape), _resident(wgu.shape), _resident(wd.shape),
                  _resident(g_next.shape)],
        out_specs=[_rows(d), _rows(d)],
        out_shape=[jax.ShapeDtypeStruct((m, d), F32), jax.ShapeDtypeStruct((m, d), F32)],
        compiler_params=_params("parallel"),
        name="ffn",
    )(x, g, wgu, wd, g_next)


def _time_mix_kernel(h_ref, tail_ref, first_ref, mix_ref, wr_ref, wk_ref, wv_ref, w0_ref, w1_ref, w2_ref,
                     a0_ref, a1_ref, a2_ref, g1_ref, g2_ref,
                     r_ref, w_ref, k_ref, v_ref, a_ref, g_ref):
    h = h_ref[...]
    batch = tail_ref.shape[0]
    before = jnp.where(pl.program_id(0) == 0, first_ref[...], tail_ref[...])
    xx = jnp.concatenate([before, h[:-batch]], axis=0) - h
    mix = mix_ref[...]

    def mixed(i):
        return (h + xx * mix[i:i + 1, :]).astype(BF16)

    def store_pairs(ref, val):
        for p in range(ref.shape[0]):
            ref[p] = val[:, p * LANES:(p + 1) * LANES]

    store_pairs(r_ref, _dot(mixed(0), wr_ref[...]))
    lw = jnp.tanh(_dot(mixed(1), w1_ref[...])).astype(BF16)
    z = -(w0_ref[...] + _dot(lw, w2_ref[...]))
    softplus = jnp.maximum(z, 0.0) + jnp.log(1.0 + jnp.exp(-jnp.abs(z)))
    store_pairs(w_ref, jnp.exp(-jnp.exp(-softplus - 0.5)))
    store_pairs(k_ref, _dot(mixed(2), wk_ref[...]))
    store_pairs(v_ref, _dot(mixed(3), wv_ref[...]))
    al = _dot(mixed(4), a1_ref[...]).astype(BF16)
    store_pairs(a_ref, _sigmoid(a0_ref[...] + _dot(al, a2_ref[...])))
    gl = _sigmoid(_dot(mixed(5), g1_ref[...])).astype(BF16)
    g_ref[...] = _dot(gl, g2_ref[...])


def _pair_rows(pairs, tile=ROW_TILE):
    return pl.BlockSpec((pairs, tile, LANES), lambda i: (0, i, 0))


def _time_mix(h, first, mix, w_r, w_k, w_v, w0, w1, w2, a0, a1, a2, g1, g2):
    m, d = h.shape
    batch = first.shape[0]
    pairs = d // LANES
    tiles_per_batch = ROW_TILE // batch
    weights = (mix, w_r, w_k, w_v, w0, w1, w2, a0, a1, a2, g1, g2)
    tail = pl.BlockSpec((batch, d), lambda i: (jnp.maximum(i * tiles_per_batch - 1, 0), 0))
    return pl.pallas_call(
        _time_mix_kernel,
        grid=(m // ROW_TILE,),
        in_specs=[_rows(d), tail, _resident(first.shape)] + [_resident(w.shape) for w in weights],
        out_specs=[_pair_rows(pairs)] * 5 + [_rows(d)],
        out_shape=[jax.ShapeDtypeStruct((pairs, m, LANES), F32)] * 5 + [jax.ShapeDtypeStruct((m, d), F32)],
        compiler_params=_params("parallel"),
        name="time_mix",
    )(h, h, first, *weights)


def _wkv_kernel(r_ref, w_ref, k_ref, v_ref, a_ref, s0_ref, kk_ref, ka_ref, rk_ref, lnw_ref, lnb_ref,
                y_ref, st_ref, s_scr, v_scr, y_scr, p_scr, *, rows_per_pair, steps, real_blocks):
    step_block = pl.program_id(1)
    n = s_scr.shape[1]
    pairs = r_ref.shape[0]

    @pl.when(step_block == 0)
    def _():
        s_scr[...] = s0_ref[...]

    def channels_by_lane(ref, off):
        tile = [ref[p, pl.ds(off, rows_per_pair), :] for p in range(pairs)]
        tile = tile[0] if pairs == 1 else jnp.concatenate(tile, axis=0)
        return tile.T

    def step(t, carry):
        off = pl.multiple_of(t * rows_per_pair, rows_per_pair)
        r2 = channels_by_lane(r_ref, off)
        w2 = channels_by_lane(w_ref, off)
        k2 = channels_by_lane(k_ref, off)
        a2 = channels_by_lane(a_ref, off)
        v_scr[...] = channels_by_lane(v_ref, off)
        for half in range(2):
            sl = slice(half * n, (half + 1) * n)
            r, w, k, a = r2[sl], w2[sl], k2[sl], a2[sl]
            kk = k * kk_ref[half]
            kk = kk / jnp.maximum(jnp.sqrt(jnp.sum(kk * kk, axis=0, keepdims=True)), 1e-12)
            b = kk * a
            k = k * (1.0 + (a - 1.0) * ka_ref[half])
            prod_prev = p_scr[half]
            prod = prod_prev * w
            p_scr[half] = prod
            inv = 1.0 / prod
            kk_s = kk * prod_prev
            b_s = b * inv
            k_s = k * inv
            r_s = r * prod
            for i in range(n):
                s_i = s_scr[half, i]
                sa = -jnp.sum(s_i * kk_s, axis=0, keepdims=True)
                v_i = v_scr[half * n + i:half * n + i + 1, :]
                s_i = s_i + sa * b_s + v_i * k_s
                s_scr[half, i] = s_i
                y_scr[half * n + i:half * n + i + 1, :] = jnp.sum(s_i * r_s, axis=0, keepdims=True)
            y = y_scr[sl, :]
            mu = jnp.mean(y, axis=0, keepdims=True)
            yc = y - mu
            var = jnp.mean(yc * yc, axis=0, keepdims=True)
            bonus = jnp.sum(r * k * rk_ref[half], axis=0, keepdims=True) * v_scr[sl, :]
            y_scr[sl, :] = yc * lax.rsqrt(var + LNX_EPS) * lnw_ref[half] + lnb_ref[half] + bonus
        y_rows = y_scr[...].T
        for p in range(pairs):
            y_ref[p, pl.ds(off, rows_per_pair), :] = y_rows[p * rows_per_pair:(p + 1) * rows_per_pair, :]
        return carry

    @pl.when(step_block < real_blocks)
    def _():
        p_scr[...] = jnp.ones(p_scr.shape, F32)
        lax.fori_loop(0, steps, step, 0)
        for half in range(2):
            prod = p_scr[half]

            def rescale(i, c2, half=half, prod=prod):
                s_scr[half, i] = s_scr[half, i] * prod
                return c2

            lax.fori_loop(0, n, rescale, 0, unroll=8)

    @pl.when(step_block >= real_blocks)
    def _():
        y_ref[...] = jnp.zeros(y_ref.shape, F32)

    @pl.when(step_block == real_blocks - 1)
    def _():
        st_ref[...] = s_scr[...]


def _wkv(r, w, k, v, a, s0, k_k, k_a, r_k, lnw, lnb, batch, real_steps, steps_per_block):
    pairs, rows, _ = r.shape
    n = RWKV_HEAD
    group = WKV_LANES // batch
    block_rows = steps_per_block * batch
    kern = functools.partial(_wkv_kernel, rows_per_pair=batch, steps=steps_per_block,
                             real_blocks=real_steps // steps_per_block)
    seq = pl.BlockSpec((group, block_rows, LANES), lambda i, j: (i, j, 0))
    state = pl.BlockSpec((None, 2, n, n, WKV_LANES), lambda i, j: (i, 0, 0, 0, 0))
    table = pl.BlockSpec((None, 2, n, WKV_LANES), lambda i, j: (i, 0, 0, 0))
    return pl.pallas_call(
        kern,
        grid=(pairs // group, rows // block_rows),
        in_specs=[seq] * 5 + [state] + [table] * 5,
        out_specs=[seq, state],
        out_shape=[jax.ShapeDtypeStruct(r.shape, F32), jax.ShapeDtypeStruct(s0.shape, F32)],
        scratch_shapes=[pltpu.VMEM((2, n, n, WKV_LANES), F32), pltpu.VMEM((2 * n, WKV_LANES), F32),
                        pltpu.VMEM((2 * n, WKV_LANES), F32), pltpu.VMEM((2, n, WKV_LANES), F32)],
        compiler_params=_params("parallel", "arbitrary"),
        name="wkv",
    )(r, w, k, v, a, s0, k_k, k_a, r_k, lnw, lnb)


def _rwkv_out_kernel(y_ref, g_ref, x_ref, wo_ref, out_ref):
    y = jnp.concatenate([y_ref[p] for p in range(y_ref.shape[0])], axis=-1)
    out_ref[...] = x_ref[...] + _dot((y * g_ref[...]).astype(BF16), wo_ref[...])


def _rwkv_out(y, g, x, w_o):
    m, d = x.shape
    return pl.pallas_call(
        _rwkv_out_kernel,
        grid=(m // ROW_TILE,),
        in_specs=[_pair_rows(y.shape[0]), _rows(d), _rows(d), _resident(w_o.shape)],
        out_specs=_rows(d),
        out_shape=jax.ShapeDtypeStruct((m, d), F32),
        compiler_params=_params("parallel"),
        name="rwkv_out",
    )(y, g, x, w_o)


def _rope_tables(pos, reps):
    half = ROPE_DIM // 2
    inv = ROPE_THETA ** (-jnp.arange(half, dtype=F32) / half)
    ang = pos.astype(F32)[:, None] * inv[None, :]
    cos, sin = jnp.cos(ang), jnp.sin(ang)
    cos = jnp.tile(jnp.concatenate([cos, cos], axis=-1), (reps, LANES // ROPE_DIM))
    sin = jnp.tile(jnp.concatenate([-sin, sin], axis=-1), (reps, LANES // ROPE_DIM))
    return cos, sin


def _swap_halves(w):
    half = w.shape[-1] // 2
    return jnp.concatenate([w[..., half:], w[..., :half]], axis=-1)


def _pad_lanes(w):
    return jnp.pad(w, [(0, 0)] * (w.ndim - 1) + [(0, LANES - w.shape[-1])])


def _time_major(x, batch):
    d = x.shape[-1]
    return x.reshape(batch, -1, d).transpose(1, 0, 2).reshape(-1, d)


def _pair_table(p, batch):
    pairs = p.shape[0] // LANES
    group = WKV_LANES // batch
    t = p.reshape(pairs // group, group, 2, RWKV_HEAD).transpose(0, 2, 3, 1)
    return jnp.repeat(t, batch, axis=-1)


def _state_to_lanes(s, batch):
    heads, n = s.shape[1], s.shape[2]
    group = WKV_LANES // batch
    s = s.reshape(batch, heads // (2 * group), group, 2, n, n)
    return s.transpose(1, 3, 4, 5, 2, 0).reshape(heads // (2 * group), 2, n, n, WKV_LANES)


def _state_from_lanes(s, batch):
    groups, _, n, _, _ = s.shape
    group = WKV_LANES // batch
    s = s.reshape(groups, 2, n, n, group, batch).transpose(5, 0, 4, 1, 2, 3)
    return s.reshape(batch, groups * group * 2, n, n)


def kernel(x_prompt, x_sample, cache_latent, cache_krope, state_shift, state_wkv, page_table, meta_tokens, ln_mix, ln_ffn, ln_final, attn_w_in, attn_q_norm, attn_kv_norm, attn_w_qb, attn_w_uk, attn_w_uv, attn_w_o, rwkv_mix, rwkv_w_r, rwkv_w_k, rwkv_w_v, rwkv_w_o, rwkv_w0, rwkv_w1, rwkv_w2, rwkv_a0, rwkv_a1, rwkv_a2, rwkv_g1, rwkv_g2, rwkv_k_k, rwkv_k_a, rwkv_r_k, rwkv_lnx_w, rwkv_lnx_b, ffn_w_gate, ffn_w_up, ffn_w_down):
    nb, seq, d = x_prompt.shape
    db, ds, _ = x_sample.shape
    t_real = seq + N_META
    t_pad = -(-t_real // SEQ_ALIGN) * SEQ_ALIGN
    past_len = page_table.shape[1] * PAGE_SIZE
    heads = d // RWKV_HEAD
    row = lambda p: p.reshape(1, -1).astype(F32)

    meta = jnp.broadcast_to(meta_tokens[None].astype(x_prompt.dtype), (nb, N_META, d))
    xp = jnp.concatenate([meta, x_prompt, jnp.zeros((nb, t_pad - t_real, d), x_prompt.dtype)], axis=1)
    xp = xp.reshape(nb * t_pad, d)
    xs = x_sample.reshape(db * ds, d)
    cos_p, sin_p = _rope_tables(jnp.arange(t_pad), nb)
    cos_s, sin_s = _rope_tables(past_len + jnp.arange(ds), db)

    w_in = attn_w_in[0]
    kr_cols = w_in[:, Q_LORA + KV_LORA:]
    w_in_ext = jnp.concatenate(
        [w_in[:, :Q_LORA + KV_LORA], _pad_lanes(kr_cols), _pad_lanes(_swap_halves(kr_cols))],
        axis=-1).astype(BF16)
    w_qb = attn_w_qb[0].reshape(Q_LORA, MLA_HEADS, NOPE_DIM + ROPE_DIM)
    q_rope_cols = w_qb[:, :, NOPE_DIM:]
    w_qb_ext = jnp.concatenate(
        [w_qb[:, :, :NOPE_DIM].reshape(Q_LORA, -1), _pad_lanes(q_rope_cols).reshape(Q_LORA, -1),
         _pad_lanes(_swap_halves(q_rope_cols)).reshape(Q_LORA, -1)], axis=-1).astype(BF16)
    w_uk_t = attn_w_uk[0].transpose(1, 2, 0).astype(BF16)
    w_uv = attn_w_uv[0].transpose(1, 0, 2).astype(BF16)
    w_o = attn_w_o[0].astype(BF16)
    mla_w = (row(ln_mix[0]), w_in_ext, row(attn_q_norm[0]), row(attn_kv_norm[0]), w_qb_ext, w_uk_t)

    q_p, kv_p, c_p, kr_p = _mla_in(xp, cos_p, sin_p, *mla_w)
    q_s, _, c_s, kr_s = _mla_in(xs, cos_s, sin_s, *mla_w)

    kv3 = kv_p.reshape(nb, t_pad, QK_DIM)
    kt4 = kv3.reshape(nb, t_pad // ATTN_Q_TOKENS, ATTN_Q_TOKENS, QK_DIM).transpose(0, 1, 3, 2)
    o_p = _attn_prompt(q_p, kt4, kv3, nb)
    q_s3 = q_s.reshape(db, ds, MLA_HEADS, QK_DIM).transpose(0, 2, 1, 3).reshape(db, MLA_HEADS * ds, QK_DIM)
    o_s = _attn_decode(page_table, q_s3, c_s.reshape(db, ds, KV_LORA), kr_s.reshape(db, ds, ROPE_DIM),
                       cache_latent[:1], jnp.swapaxes(cache_krope[:1], 2, 3))
    o_s = o_s.reshape(db, MLA_HEADS, ds, KV_LORA).transpose(0, 2, 1, 3).reshape(db * ds, MLA_HEADS * KV_LORA)

    def ffn_weights(i):
        n_chunks = ffn_w_gate.shape[2] // FFN_CHUNK
        wg = ffn_w_gate[i].reshape(d, n_chunks, FFN_CHUNK)
        wu = ffn_w_up[i].reshape(d, n_chunks, FFN_CHUNK)
        wgu = jnp.concatenate([wg, wu], axis=-1).transpose(1, 0, 2).astype(BF16)
        wd = ffn_w_down[i].reshape(n_chunks, FFN_CHUNK, d).astype(BF16)
        return wgu, wd

    ffn0 = (row(ln_ffn[0]),) + ffn_weights(0) + (row(ln_mix[1]),)
    xp, hp = _ffn(_mla_out(o_p, xp, w_uv, w_o), *ffn0)
    xs, hs = _ffn(_mla_out(o_s, xs, w_uv, w_o), *ffn0)

    xp, hp, xs, hs = _time_major(xp, nb), _time_major(hp, nb), _time_major(xs, db), _time_major(hs, db)
    shift_p = jnp.zeros((nb, d), F32)
    shift_s = state_shift[0].astype(F32)

    def lora(w_down, w_up):
        rank = w_down.shape[1]
        pad = -(-rank // LANES) * LANES - rank
        return (jnp.pad(w_down, ((0, 0), (0, pad))).astype(BF16),
                jnp.pad(w_up, ((0, pad), (0, 0))).astype(BF16))

    w1, w2 = lora(rwkv_w1[0], rwkv_w2[0])
    a1, a2 = lora(rwkv_a1[0], rwkv_a2[0])
    g1, g2 = lora(rwkv_g1[0], rwkv_g2[0])
    tm_w = (rwkv_mix[0].astype(F32), rwkv_w_r[0].astype(BF16), rwkv_w_k[0].astype(BF16),
            rwkv_w_v[0].astype(BF16), row(rwkv_w0[0]), w1, w2, row(rwkv_a0[0]), a1, a2, g1, g2)
    tables = (rwkv_k_k[0], rwkv_k_a[0], rwkv_r_k[0].reshape(-1), rwkv_lnx_w[0], rwkv_lnx_b[0])
    w_o1 = rwkv_w_o[0].astype(BF16)

    def time_mixing(h, shift, x, batch, steps, s0, steps_per_block):
        r, w, k, v, a, g = _time_mix(h, shift, *tm_w)
        tabs = [_pair_table(p.astype(F32), batch) for p in tables]
        y, s_t = _wkv(r, w, k, v, a, _state_to_lanes(s0, batch), *tabs, batch, steps, steps_per_block)
        return _rwkv_out(y, g, x, w_o1), _state_from_lanes(s_t, batch)

    s0_p = jnp.zeros((nb, heads, RWKV_HEAD, RWKV_HEAD), F32)
    xp, wkv_p = time_mixing(hp, shift_p, xp, nb, t_real, s0_p, WKV_STEPS)
    xs, wkv_s = time_mixing(hs, shift_s, xs, db, ds, state_wkv[0].astype(F32), ds)

    ffn1 = (row(ln_ffn[1]),) + ffn_weights(1) + (row(ln_final),)
    _, yp = _ffn(xp, *ffn1)
    _, ys = _ffn(xs, *ffn1)

    seq3 = lambda u: u.reshape(nb, t_pad, -1)[:, :t_real]
    return (yp.reshape(t_pad, nb, d)[N_META:t_real].transpose(1, 0, 2),
            ys.reshape(ds, db, d).transpose(1, 0, 2),
            seq3(c_p)[None],
            seq3(kr_p)[None],
            hp[(t_real - 1) * nb:t_real * nb][None],
            wkv_p.astype(state_wkv.dtype)[None],
            c_s.reshape(db, ds, KV_LORA)[None],
            kr_s.reshape(db, ds, ROPE_DIM)[None],
            hs[(ds - 1) * db:][None],
            wkv_s.astype(state_wkv.dtype)[None])
```

```python
import functools

import jax
import jax.numpy as jnp
from jax import lax
from jax.experimental import pallas as pl
from jax.experimental.pallas import tpu as pltpu

F32 = jnp.float32
BF16 = jnp.bfloat16

N_META = 16
RMS_EPS = 1e-6
MLA_HEADS = 8
Q_LORA = 512
KV_LORA = 256
NOPE_DIM = 128
ROPE_DIM = 64
V_DIM = 128
ROPE_THETA = 10000.0
SOFTMAX_SCALE = (NOPE_DIM + ROPE_DIM) ** -0.5
PAGE_SIZE = 128
RWKV_HEAD = 64
LNX_EPS = 64e-5
MASK_VALUE = -1e30

LANES = 128
QK_DIM = KV_LORA + LANES
ROW_TILE = 512
SEQ_ALIGN = 128
ATTN_Q_TOKENS = 128
ATTN_ROW_CHUNK = 512
ATTN_K_GROUP = 8
DECODE_PAGES = 32
DECODE_GROUPS = 2
FFN_CHUNK = 256
WKV_LANES = 128
WKV_STEPS = 16
VMEM_LIMIT = 56 * 1024 * 1024


def _params(*sem):
    return pltpu.CompilerParams(dimension_semantics=sem, vmem_limit_bytes=VMEM_LIMIT)


def _resident(shape):
    nd = len(shape)
    return pl.BlockSpec(shape, lambda *_: (0,) * nd, pipeline_mode=pl.Buffered(1))


def _rows(width, tile=ROW_TILE):
    return pl.BlockSpec((tile, width), lambda i: (i, 0))


def _rms(x, g):
    return x * lax.rsqrt(jnp.mean(x * x, axis=-1, keepdims=True) + RMS_EPS) * g


def _sigmoid(x):
    return 1.0 / (1.0 + jnp.exp(-x))


def _dot(a, b):
    return jnp.dot(a, b, preferred_element_type=F32)


def _dot_nt(a, b):
    return lax.dot_general(a, b, (((1,), (1,)), ((), ())), preferred_element_type=F32)


def _mla_in_kernel(x_ref, cos_ref, sin_ref, g_ref, win_ref, qn_ref, kvn_ref, wqb_ref, wuk_ref,
                   q_ref, kv_ref, c_ref, kr_ref):
    h = _rms(x_ref[...], g_ref[...]).astype(BF16)
    z = _dot(h, win_ref[...])
    cos = cos_ref[...]
    sin = sin_ref[...]
    c = _rms(z[:, Q_LORA:Q_LORA + KV_LORA], kvn_ref[...])
    kr_lo = Q_LORA + KV_LORA
    kr = z[:, kr_lo:kr_lo + LANES] * cos + z[:, kr_lo + LANES:kr_lo + 2 * LANES] * sin
    c_ref[...] = c
    kr_ref[...] = kr[:, :ROPE_DIM]
    kv_ref[:, :KV_LORA] = c.astype(BF16)
    kv_ref[:, KV_LORA:] = kr.astype(BF16)
    qn = _rms(z[:, :Q_LORA], qn_ref[...]).astype(BF16)
    q = _dot(qn, wqb_ref[...])
    rope_lo = MLA_HEADS * NOPE_DIM
    swap_lo = rope_lo + MLA_HEADS * LANES
    for hh in range(MLA_HEADS):
        ql = _dot(q[:, hh * NOPE_DIM:(hh + 1) * NOPE_DIM].astype(BF16), wuk_ref[hh])
        q_ref[:, hh * QK_DIM:hh * QK_DIM + KV_LORA] = (ql * SOFTMAX_SCALE).astype(BF16)
        qr = (q[:, rope_lo + hh * LANES:rope_lo + (hh + 1) * LANES] * cos
              + q[:, swap_lo + hh * LANES:swap_lo + (hh + 1) * LANES] * sin)
        q_ref[:, hh * QK_DIM + KV_LORA:(hh + 1) * QK_DIM] = (qr * SOFTMAX_SCALE).astype(BF16)


def _mla_in(x, cos, sin, g, w_in, q_norm, kv_norm, w_qb, w_uk):
    m = x.shape[0]
    d = x.shape[1]
    return pl.pallas_call(
        _mla_in_kernel,
        grid=(m // ROW_TILE,),
        in_specs=[_rows(d), _rows(LANES), _rows(LANES), _resident(g.shape), _resident(w_in.shape),
                  _resident(q_norm.shape), _resident(kv_norm.shape), _resident(w_qb.shape),
                  _resident(w_uk.shape)],
        out_specs=[_rows(MLA_HEADS * QK_DIM), _rows(QK_DIM), _rows(KV_LORA), _rows(ROPE_DIM)],
        out_shape=[jax.ShapeDtypeStruct((m, MLA_HEADS * QK_DIM), BF16),
                   jax.ShapeDtypeStruct((m, QK_DIM), BF16),
                   jax.ShapeDtypeStruct((m, KV_LORA), F32),
                   jax.ShapeDtypeStruct((m, ROPE_DIM), F32)],
        compiler_params=_params("parallel"),
        name="mla_in",
    )(x, cos, sin, g, w_in, q_norm, kv_norm, w_qb, w_uk)


def _attn_prompt_kernel(q_ref, kt_ref, v_ref, o_ref, q_scr, m_scr, l_scr, acc_scr):
    qi = pl.program_id(1)
    tq = ATTN_Q_TOKENS
    rows = tq * MLA_HEADS
    for hh in range(MLA_HEADS):
        q_scr[hh * tq:(hh + 1) * tq, :] = q_ref[:, hh * QK_DIM:(hh + 1) * QK_DIM]
    m_scr[...] = jnp.full(m_scr.shape, MASK_VALUE, F32)
    l_scr[...] = jnp.zeros(l_scr.shape, F32)
    acc_scr[...] = jnp.zeros(acc_scr.shape, F32)

    def process(first_block, n_blocks, diagonal):
        kt = [kt_ref[first_block + n] for n in range(n_blocks)]
        kt = kt[0] if n_blocks == 1 else jnp.concatenate(kt, axis=-1)
        start = pl.multiple_of(first_block * tq, tq)
        v = v_ref[pl.ds(start, n_blocks * tq), :KV_LORA]
        for r0 in range(0, rows, ATTN_ROW_CHUNK):
            sl = slice(r0, r0 + ATTN_ROW_CHUNK)
            s = _dot(q_scr[sl, :], kt)
            if diagonal:
                t_loc = (r0 + lax.broadcasted_iota(jnp.int32, (ATTN_ROW_CHUNK, 1), 0)) % tq
                col = lax.broadcasted_iota(jnp.int32, (1, n_blocks * tq), 1) - (n_blocks - 1) * tq
                s = jnp.where(col <= t_loc, s, MASK_VALUE)
            tiles = [s[:, n * tq:(n + 1) * tq] for n in range(n_blocks)]
            tile_max = functools.reduce(jnp.maximum, tiles)
            m_prev = m_scr[sl, :]
            m_new = jnp.maximum(m_prev, jnp.max(tile_max, axis=-1, keepdims=True))
            alpha = jnp.exp(m_prev - m_new)
            ps = [jnp.exp(tile - m_new) for tile in tiles]
            l_scr[sl, :] = alpha * l_scr[sl, :] + functools.reduce(jnp.add, ps)
            p = ps[0] if n_blocks == 1 else jnp.concatenate(ps, axis=-1)
            alpha_wide = jnp.concatenate([alpha] * (KV_LORA // tq), axis=-1)
            acc_scr[sl, :] = alpha_wide * acc_scr[sl, :] + _dot(p.astype(BF16), v)
            m_scr[sl, :] = m_new

    def quad(kj, carry):
        process(ATTN_K_GROUP * kj, ATTN_K_GROUP, False)
        return carry

    lax.fori_loop(0, qi // ATTN_K_GROUP, quad, 0)
    rest = qi % ATTN_K_GROUP
    for n in range(ATTN_K_GROUP):
        pl.when(rest == n)(functools.partial(process, qi - n, n + 1, True))
    for hh in range(MLA_HEADS):
        sl = slice(hh * tq, (hh + 1) * tq)
        l_row = jnp.sum(l_scr[sl, :], axis=-1, keepdims=True)
        o_ref[:, hh * KV_LORA:(hh + 1) * KV_LORA] = (acc_scr[sl, :] / l_row).astype(BF16)


def _attn_prompt(q, kt, kv, n_seq):
    blocks = kt.shape[1]
    t = kv.shape[1]
    tq = ATTN_Q_TOKENS
    rows = tq * MLA_HEADS
    return pl.pallas_call(
        _attn_prompt_kernel,
        grid=(n_seq, blocks),
        in_specs=[pl.BlockSpec((tq, MLA_HEADS * QK_DIM), lambda i, j: (i * blocks + j, 0)),
                  pl.BlockSpec((None, blocks, QK_DIM, tq), lambda i, j: (i, 0, 0, 0)),
                  pl.BlockSpec((None, t, QK_DIM), lambda i, j: (i, 0, 0))],
        out_specs=pl.BlockSpec((tq, MLA_HEADS * KV_LORA), lambda i, j: (i * blocks + j, 0)),
        out_shape=jax.ShapeDtypeStruct((q.shape[0], MLA_HEADS * KV_LORA), BF16),
        scratch_shapes=[pltpu.VMEM((rows, QK_DIM), BF16), pltpu.VMEM((rows, tq), F32),
                        pltpu.VMEM((rows, tq), F32), pltpu.VMEM((rows, KV_LORA), F32)],
        compiler_params=_params("parallel", "arbitrary"),
        name="attn_prompt",
    )(q, kt, kv)


def _attn_decode_kernel(pt_ref, q_ref, cn_ref, krn_ref, lat_hbm, rope_hbm, o_ref,
                        lat_buf, rope_buf, sem, *, n_new, n_groups):
    b = pl.program_id(0)
    q = q_ref[...]
    ql = q[:, :KV_LORA]
    qr = q[:, KV_LORA:KV_LORA + ROPE_DIM]
    rows = q.shape[0]

    def page_copies(page, slot, i):
        return (pltpu.make_async_copy(lat_hbm.at[0, page], lat_buf.at[slot, i], sem.at[slot, 0]),
                pltpu.make_async_copy(rope_hbm.at[0, page], rope_buf.at[slot, i], sem.at[slot, 1]))

    def start_group(row, group, slot):
        for i in range(DECODE_PAGES):
            for cp in page_copies(pt_ref[row, group * DECODE_PAGES + i], slot, i):
                cp.start()

    def wait_group(slot):
        for i in range(DECODE_PAGES):
            for cp in page_copies(0, slot, i):
                cp.wait()

    @pl.when(b == 0)
    def _():
        start_group(0, 0, 0)

    qlf = ql.astype(F32)
    qrf = qr.astype(F32)
    cn = cn_ref[...].astype(BF16).astype(F32)
    krn = krn_ref[...].astype(BF16).astype(F32)
    q_tok = lax.broadcasted_iota(jnp.int32, (rows, 1), 0) % n_new
    cols = []
    for t in range(n_new):
        s_t = (jnp.sum(qlf * cn[t:t + 1, :], axis=-1, keepdims=True)
               + jnp.sum(qrf * krn[t:t + 1, :], axis=-1, keepdims=True))
        cols.append(jnp.where(q_tok >= t, s_t, MASK_VALUE))
    m_run = functools.reduce(jnp.maximum, cols)
    l_run = jnp.zeros((rows, 1), F32)
    acc = jnp.zeros((rows, KV_LORA), F32)
    for t in range(n_new):
        p_t = jnp.exp(cols[t] - m_run)
        l_run = l_run + p_t
        acc = acc + p_t.astype(BF16).astype(F32) * cn[t:t + 1, :]

    per_part = DECODE_PAGES // DECODE_GROUPS
    for group in range(n_groups):
        slot = group % 2
        if group + 1 < n_groups:
            start_group(b, group + 1, 1 - slot)
        else:
            pl.when(b + 1 < pl.num_programs(0))(functools.partial(start_group, b + 1, 0, 1 - slot))
        wait_group(slot)
        partials = []
        for part in range(DECODE_GROUPS):
            idx = range(part * per_part, (part + 1) * per_part)
            kbs = [lat_buf[slot, i].astype(BF16) for i in idx]
            s = jnp.concatenate(
                [_dot_nt(ql, kb) + _dot(qr, rope_buf[slot, i].astype(BF16)) for i, kb in zip(idx, kbs)],
                axis=-1)
            m_g = jnp.max(s, axis=-1, keepdims=True)
            p = jnp.exp(s - m_g)
            l_g = jnp.sum(p, axis=-1, keepdims=True)
            pb = p.astype(BF16)
            acc_g = _dot(pb[:, :PAGE_SIZE], kbs[0])
            for n in range(1, per_part):
                acc_g = acc_g + _dot(pb[:, n * PAGE_SIZE:(n + 1) * PAGE_SIZE], kbs[n])
            partials.append((m_g, l_g, acc_g))
        m_new = functools.reduce(jnp.maximum, [m_g for m_g, _, _ in partials], m_run)
        alpha = jnp.exp(m_run - m_new)
        l_run = alpha * l_run
        acc = alpha * acc
        for m_g, l_g, acc_g in partials:
            w_g = jnp.exp(m_g - m_new)
            l_run = l_run + w_g * l_g
            acc = acc + w_g * acc_g
        m_run = m_new
    o_ref[...] = (acc / l_run).astype(BF16)


def _attn_decode(page_table, q, c_new, kr_new, cache_latent, cache_krope_t):
    db, rows, _ = q.shape
    n_new = c_new.shape[1]
    n_groups = page_table.shape[1] // DECODE_PAGES
    assert n_groups % 2 == 0, "the prefetch of the next row's first group assumes it lands in slot 0"
    grid_spec = pltpu.PrefetchScalarGridSpec(
        num_scalar_prefetch=1,
        grid=(db,),
        in_specs=[pl.BlockSpec((None, rows, QK_DIM), lambda b, pt: (b, 0, 0)),
                  pl.BlockSpec((None, n_new, KV_LORA), lambda b, pt: (b, 0, 0)),
                  pl.BlockSpec((None, n_new, ROPE_DIM), lambda b, pt: (b, 0, 0)),
                  pl.BlockSpec(memory_space=pl.ANY),
                  pl.BlockSpec(memory_space=pl.ANY)],
        out_specs=pl.BlockSpec((None, rows, KV_LORA), lambda b, pt: (b, 0, 0)),
        scratch_shapes=[pltpu.VMEM((2, DECODE_PAGES, PAGE_SIZE, KV_LORA), F32),
                        pltpu.VMEM((2, DECODE_PAGES, ROPE_DIM, PAGE_SIZE), F32),
                        pltpu.SemaphoreType.DMA((2, 2))],
    )
    return pl.pallas_call(
        functools.partial(_attn_decode_kernel, n_new=n_new, n_groups=n_groups),
        grid_spec=grid_spec,
        out_shape=jax.ShapeDtypeStruct((db, rows, KV_LORA), BF16),
        compiler_params=_params("arbitrary"),
        name="attn_decode",
    )(page_table, q, c_new, kr_new, cache_latent, cache_krope_t)


def _mla_out_kernel(o_ref, x_ref, wuv_ref, wo_ref, out_ref):
    us = [_dot(o_ref[:, hh * KV_LORA:(hh + 1) * KV_LORA], wuv_ref[hh]).astype(BF16)
          for hh in range(MLA_HEADS)]
    out_ref[...] = x_ref[...] + _dot(jnp.concatenate(us, axis=-1), wo_ref[...])


def _mla_out(o, x, w_uv, w_o):
    m, d = x.shape
    return pl.pallas_call(
        _mla_out_kernel,
        grid=(m // ROW_TILE,),
        in_specs=[_rows(o.shape[1]), _rows(d), _resident(w_uv.shape), _resident(w_o.shape)],
        out_specs=_rows(d),
        out_shape=jax.ShapeDtypeStruct((m, d), F32),
        compiler_params=_params("parallel"),
        name="mla_out",
    )(o, x, w_uv, w_o)


def _ffn_kernel(x_ref, g_ref, wgu_ref, wd_ref, gn_ref, x_out_ref, hn_ref):
    x = x_ref[...]
    h = _rms(x, g_ref[...]).astype(BF16)
    acc = x
    for ci in range(wgu_ref.shape[0]):
        gu = _dot(h, wgu_ref[ci])
        gate = gu[:, :FFN_CHUNK]
        act = (gate * _sigmoid(gate) * gu[:, FFN_CHUNK:]).astype(BF16)
        acc = acc + _dot(act, wd_ref[ci])
    x_out_ref[...] = acc
    hn_ref[...] = _rms(acc, gn_ref[...])


def _ffn(x, g, wgu, wd, g_next):
    m, d = x.shape
    return pl.pallas_call(
        _ffn_kernel,
        grid=(m // ROW_TILE,),
        in_specs=[_rows(d), _resident(g.shape), _resident(wgu.shape), _resident(wd.shape),
                  _resident(g_next.shape)],
        out_specs=[_rows(d), _rows(d)],
        out_shape=[jax.ShapeDtypeStruct((m, d), F32), jax.ShapeDtypeStruct((m, d), F32)],
        compiler_params=_params("parallel"),
        name="ffn",
    )(x, g, wgu, wd, g_next)


def _time_mix_kernel(h_ref, tail_ref, first_ref, mix_ref, wr_ref, wk_ref, wv_ref, w0_ref, w1_ref, w2_ref,
                     a0_ref, a1_ref, a2_ref, g1_ref, g2_ref,
                     r_ref, w_ref, k_ref, v_ref, a_ref, g_ref):
    h = h_ref[...]
    batch = tail_ref.shape[0]
    before = jnp.where(pl.program_id(0) == 0, first_ref[...], tail_ref[...])
    xx = jnp.concatenate([before, h[:-batch]], axis=0) - h
    mix = mix_ref[...]

    def mixed(i):
        return (h + xx * mix[i:i + 1, :]).astype(BF16)

    def store_pairs(ref, val):
        for p in range(ref.shape[0]):
            ref[p] = val[:, p * LANES:(p + 1) * LANES]

    store_pairs(r_ref, _dot(mixed(0), wr_ref[...]))
    lw = jnp.tanh(_dot(mixed(1), w1_ref[...])).astype(BF16)
    z = -(w0_ref[...] + _dot(lw, w2_ref[...]))
    softplus = jnp.maximum(z, 0.0) + jnp.log(1.0 + jnp.exp(-jnp.abs(z)))
    store_pairs(w_ref, jnp.exp(-jnp.exp(-softplus - 0.5)))
    store_pairs(k_ref, _dot(mixed(2), wk_ref[...]))
    store_pairs(v_ref, _dot(mixed(3), wv_ref[...]))
    al = _dot(mixed(4), a1_ref[...]).astype(BF16)
    store_pairs(a_ref, _sigmoid(a0_ref[...] + _dot(al, a2_ref[...])))
    gl = _sigmoid(_dot(mixed(5), g1_ref[...])).astype(BF16)
    g_ref[...] = _dot(gl, g2_ref[...])


def _pair_rows(pairs, tile=ROW_TILE):
    return pl.BlockSpec((pairs, tile, LANES), lambda i: (0, i, 0))


def _time_mix(h, first, mix, w_r, w_k, w_v, w0, w1, w2, a0, a1, a2, g1, g2):
    m, d = h.shape
    batch = first.shape[0]
    pairs = d // LANES
    tiles_per_batch = ROW_TILE // batch
    weights = (mix, w_r, w_k, w_v, w0, w1, w2, a0, a1, a2, g1, g2)
    tail = pl.BlockSpec((batch, d), lambda i: (jnp.maximum(i * tiles_per_batch - 1, 0), 0))
    return pl.pallas_call(
        _time_mix_kernel,
        grid=(m // ROW_TILE,),
        in_specs=[_rows(d), tail, _resident(first.shape)] + [_resident(w.shape) for w in weights],
        out_specs=[_pair_rows(pairs)] * 5 + [_rows(d)],
        out_shape=[jax.ShapeDtypeStruct((pairs, m, LANES), F32)] * 5 + [jax.ShapeDtypeStruct((m, d), F32)],
        compiler_params=_params("parallel"),
        name="time_mix",
    )(h, h, first, *weights)


def _wkv_kernel(r_ref, w_ref, k_ref, v_ref, a_ref, s0_ref, kk_ref, ka_ref, rk_ref, lnw_ref, lnb_ref,
                y_ref, st_ref, s_scr, v_scr, y_scr, p_scr, c_scr, *, rows_per_pair, steps, real_blocks):
    step_block = pl.program_id(1)
    n = s_scr.shape[1]
    pairs = r_ref.shape[0]

    @pl.when(step_block == 0)
    def _():
        s_scr[...] = s0_ref[...]

    def channels_by_lane(ref, off):
        tile = [ref[p, pl.ds(off, rows_per_pair), :] for p in range(pairs)]
        tile = tile[0] if pairs == 1 else jnp.concatenate(tile, axis=0)
        return tile.T

    def step(t, carry):
        off = pl.multiple_of(t * rows_per_pair, rows_per_pair)
        r2 = channels_by_lane(r_ref, off)
        w2 = channels_by_lane(w_ref, off)
        k2 = channels_by_lane(k_ref, off)
        a2 = channels_by_lane(a_ref, off)
        v_scr[...] = channels_by_lane(v_ref, off)
        for half in range(2):
            sl = slice(half * n, (half + 1) * n)
            r, w, k, a = r2[sl], w2[sl], k2[sl], a2[sl]
            kk = k * kk_ref[half]
            kk = kk / jnp.maximum(jnp.sqrt(jnp.sum(kk * kk, axis=0, keepdims=True)), 1e-12)
            b = kk * a
            k = k * (1.0 + (a - 1.0) * ka_ref[half])
            prod_prev = p_scr[half]
            prod = prod_prev * w
            p_scr[half] = prod
            inv = 1.0 / prod
            c_scr[0] = kk * prod_prev
            c_scr[1] = b * inv
            c_scr[2] = k * inv
            c_scr[3] = r * prod
            v = v_scr[sl, :]
            sa = jnp.zeros_like(v)
            for j in range(n):
                sa = sa - s_scr[half, j] * c_scr[0, j:j + 1, :]
            y = jnp.zeros_like(v)
            for j in range(n):
                s_j = s_scr[half, j] + sa * c_scr[1, j:j + 1, :] + v * c_scr[2, j:j + 1, :]
                s_scr[half, j] = s_j
                y = y + s_j * c_scr[3, j:j + 1, :]
            mu = jnp.mean(y, axis=0, keepdims=True)
            yc = y - mu
            var = jnp.mean(yc * yc, axis=0, keepdims=True)
            bonus = jnp.sum(r * k * rk_ref[half], axis=0, keepdims=True) * v_scr[sl, :]
            y_scr[sl, :] = yc * lax.rsqrt(var + LNX_EPS) * lnw_ref[half] + lnb_ref[half] + bonus
        y_rows = y_scr[...].T
        for p in range(pairs):
            y_ref[p, pl.ds(off, rows_per_pair), :] = y_rows[p * rows_per_pair:(p + 1) * rows_per_pair, :]
        return carry

    @pl.when(step_block < real_blocks)
    def _():
        p_scr[...] = jnp.ones(p_scr.shape, F32)
        lax.fori_loop(0, steps, step, 0)
        for half in range(2):

            def rescale(j, c2, half=half):
                s_scr[half, j] = s_scr[half, j] * p_scr[half, pl.ds(j, 1), :]
                return c2

            lax.fori_loop(0, n, rescale, 0, unroll=8)

    @pl.when(step_block >= real_blocks)
    def _():
        y_ref[...] = jnp.zeros(y_ref.shape, F32)

    @pl.when(step_block == real_blocks - 1)
    def _():
        st_ref[...] = s_scr[...]


def _wkv(r, w, k, v, a, s0, k_k, k_a, r_k, lnw, lnb, batch, real_steps, steps_per_block):
    pairs, rows, _ = r.shape
    n = RWKV_HEAD
    group = WKV_LANES // batch
    block_rows = steps_per_block * batch
    kern = functools.partial(_wkv_kernel, rows_per_pair=batch, steps=steps_per_block,
                             real_blocks=real_steps // steps_per_block)
    seq = pl.BlockSpec((group, block_rows, LANES), lambda i, j: (i, j, 0))
    state = pl.BlockSpec((None, 2, n, n, WKV_LANES), lambda i, j: (i, 0, 0, 0, 0))
    table = pl.BlockSpec((None, 2, n, WKV_LANES), lambda i, j: (i, 0, 0, 0))
    return pl.pallas_call(
        kern,
        grid=(pairs // group, rows // block_rows),
        in_specs=[seq] * 5 + [state] + [table] * 5,
        out_specs=[seq, state],
        out_shape=[jax.ShapeDtypeStruct(r.shape, F32), jax.ShapeDtypeStruct(s0.shape, F32)],
        scratch_shapes=[pltpu.VMEM((2, n, n, WKV_LANES), F32), pltpu.VMEM((2 * n, WKV_LANES), F32),
                        pltpu.VMEM((2 * n, WKV_LANES), F32), pltpu.VMEM((2, n, WKV_LANES), F32),
                        pltpu.VMEM((4, n, WKV_LANES), F32)],
        compiler_params=_params("parallel", "arbitrary"),
        name="wkv",
    )(r, w, k, v, a, s0, k_k, k_a, r_k, lnw, lnb)


def _rwkv_out_kernel(y_ref, g_ref, x_ref, wo_ref, out_ref):
    y = jnp.concatenate([y_ref[p] for p in range(y_ref.shape[0])], axis=-1)
    out_ref[...] = x_ref[...] + _dot((y * g_ref[...]).astype(BF16), wo_ref[...])


def _rwkv_out(y, g, x, w_o):
    m, d = x.shape
    return pl.pallas_call(
        _rwkv_out_kernel,
        grid=(m // ROW_TILE,),
        in_specs=[_pair_rows(y.shape[0]), _rows(d), _rows(d), _resident(w_o.shape)],
        out_specs=_rows(d),
        out_shape=jax.ShapeDtypeStruct((m, d), F32),
        compiler_params=_params("parallel"),
        name="rwkv_out",
    )(y, g, x, w_o)


def _rope_tables(pos, reps):
    half = ROPE_DIM // 2
    inv = ROPE_THETA ** (-jnp.arange(half, dtype=F32) / half)
    ang = pos.astype(F32)[:, None] * inv[None, :]
    cos, sin = jnp.cos(ang), jnp.sin(ang)
    cos = jnp.tile(jnp.concatenate([cos, cos], axis=-1), (reps, LANES // ROPE_DIM))
    sin = jnp.tile(jnp.concatenate([-sin, sin], axis=-1), (reps, LANES // ROPE_DIM))
    return cos, sin


def _swap_halves(w):
    half = w.shape[-1] // 2
    return jnp.concatenate([w[..., half:], w[..., :half]], axis=-1)


def _pad_lanes(w):
    return jnp.pad(w, [(0, 0)] * (w.ndim - 1) + [(0, LANES - w.shape[-1])])


def _time_major(x, batch):
    d = x.shape[-1]
    return x.reshape(batch, -1, d).transpose(1, 0, 2).reshape(-1, d)


def _pair_table(p, batch):
    pairs = p.shape[0] // LANES
    group = WKV_LANES // batch
    t = p.reshape(pairs // group, group, 2, RWKV_HEAD).transpose(0, 2, 3, 1)
    return jnp.repeat(t, batch, axis=-1)


def _state_to_lanes(s, batch):
    heads, n = s.shape[1], s.shape[2]
    group = WKV_LANES // batch
    s = s.reshape(batch, heads // (2 * group), group, 2, n, n)
    return s.transpose(1, 3, 5, 4, 2, 0).reshape(heads // (2 * group), 2, n, n, WKV_LANES)


def _state_from_lanes(s, batch):
    groups, _, n, _, _ = s.shape
    group = WKV_LANES // batch
    s = s.reshape(groups, 2, n, n, group, batch).transpose(5, 0, 4, 1, 3, 2)
    return s.reshape(batch, groups * group * 2, n, n)


def kernel(x_prompt, x_sample, cache_latent, cache_krope, state_shift, state_wkv, page_table, meta_tokens, ln_mix, ln_ffn, ln_final, attn_w_in, attn_q_norm, attn_kv_norm, attn_w_qb, attn_w_uk, attn_w_uv, attn_w_o, rwkv_mix, rwkv_w_r, rwkv_w_k, rwkv_w_v, rwkv_w_o, rwkv_w0, rwkv_w1, rwkv_w2, rwkv_a0, rwkv_a1, rwkv_a2, rwkv_g1, rwkv_g2, rwkv_k_k, rwkv_k_a, rwkv_r_k, rwkv_lnx_w, rwkv_lnx_b, ffn_w_gate, ffn_w_up, ffn_w_down):
    nb, seq, d = x_prompt.shape
    db, ds, _ = x_sample.shape
    t_real = seq + N_META
    t_pad = -(-t_real // SEQ_ALIGN) * SEQ_ALIGN
    past_len = page_table.shape[1] * PAGE_SIZE
    heads = d // RWKV_HEAD
    row = lambda p: p.reshape(1, -1).astype(F32)

    meta = jnp.broadcast_to(meta_tokens[None].astype(x_prompt.dtype), (nb, N_META, d))
    xp = jnp.concatenate([meta, x_prompt, jnp.zeros((nb, t_pad - t_real, d), x_prompt.dtype)], axis=1)
    xp = xp.reshape(nb * t_pad, d)
    xs = x_sample.reshape(db * ds, d)
    cos_p, sin_p = _rope_tables(jnp.arange(t_pad), nb)
    cos_s, sin_s = _rope_tables(past_len + jnp.arange(ds), db)

    w_in = attn_w_in[0]
    kr_cols = w_in[:, Q_LORA + KV_LORA:]
    w_in_ext = jnp.concatenate(
        [w_in[:, :Q_LORA + KV_LORA], _pad_lanes(kr_cols), _pad_lanes(_swap_halves(kr_cols))],
        axis=-1).astype(BF16)
    w_qb = attn_w_qb[0].reshape(Q_LORA, MLA_HEADS, NOPE_DIM + ROPE_DIM)
    q_rope_cols = w_qb[:, :, NOPE_DIM:]
    w_qb_ext = jnp.concatenate(
        [w_qb[:, :, :NOPE_DIM].reshape(Q_LORA, -1), _pad_lanes(q_rope_cols).reshape(Q_LORA, -1),
         _pad_lanes(_swap_halves(q_rope_cols)).reshape(Q_LORA, -1)], axis=-1).astype(BF16)
    w_uk_t = attn_w_uk[0].transpose(1, 2, 0).astype(BF16)
    w_uv = attn_w_uv[0].transpose(1, 0, 2).astype(BF16)
    w_o = attn_w_o[0].astype(BF16)
    mla_w = (row(ln_mix[0]), w_in_ext, row(attn_q_norm[0]), row(attn_kv_norm[0]), w_qb_ext, w_uk_t)

    q_p, kv_p, c_p, kr_p = _mla_in(xp, cos_p, sin_p, *mla_w)
    q_s, _, c_s, kr_s = _mla_in(xs, cos_s, sin_s, *mla_w)

    kv3 = kv_p.reshape(nb, t_pad, QK_DIM)
    kt4 = kv3.reshape(nb, t_pad // ATTN_Q_TOKENS, ATTN_Q_TOKENS, QK_DIM).transpose(0, 1, 3, 2)
    o_p = _attn_prompt(q_p, kt4, kv3, nb)
    q_s3 = q_s.reshape(db, ds, MLA_HEADS, QK_DIM).transpose(0, 2, 1, 3).reshape(db, MLA_HEADS * ds, QK_DIM)
    o_s = _attn_decode(page_table, q_s3, c_s.reshape(db, ds, KV_LORA), kr_s.reshape(db, ds, ROPE_DIM),
                       cache_latent[:1], jnp.swapaxes(cache_krope[:1], 2, 3))
    o_s = o_s.reshape(db, MLA_HEADS, ds, KV_LORA).transpose(0, 2, 1, 3).reshape(db * ds, MLA_HEADS * KV_LORA)

    def ffn_weights(i):
        n_chunks = ffn_w_gate.shape[2] // FFN_CHUNK
        wg = ffn_w_gate[i].reshape(d, n_chunks, FFN_CHUNK)
        wu = ffn_w_up[i].reshape(d, n_chunks, FFN_CHUNK)
        wgu = jnp.concatenate([wg, wu], axis=-1).transpose(1, 0, 2).astype(BF16)
        wd = ffn_w_down[i].reshape(n_chunks, FFN_CHUNK, d).astype(BF16)
        return wgu, wd

    ffn0 = (row(ln_ffn[0]),) + ffn_weights(0) + (row(ln_mix[1]),)
    xp, hp = _ffn(_mla_out(o_p, xp, w_uv, w_o), *ffn0)
    xs, hs = _ffn(_mla_out(o_s, xs, w_uv, w_o), *ffn0)

    xp, hp, xs, hs = _time_major(xp, nb), _time_major(hp, nb), _time_major(xs, db), _time_major(hs, db)
    shift_p = jnp.zeros((nb, d), F32)
    shift_s = state_shift[0].astype(F32)

    def lora(w_down, w_up):
        rank = w_down.shape[1]
        pad = -(-rank // LANES) * LANES - rank
        return (jnp.pad(w_down, ((0, 0), (0, pad))).astype(BF16),
                jnp.pad(w_up, ((0, pad), (0, 0))).astype(BF16))

    w1, w2 = lora(rwkv_w1[0], rwkv_w2[0])
    a1, a2 = lora(rwkv_a1[0], rwkv_a2[0])
    g1, g2 = lora(rwkv_g1[0], rwkv_g2[0])
    tm_w = (rwkv_mix[0].astype(F32), rwkv_w_r[0].astype(BF16), rwkv_w_k[0].astype(BF16),
            rwkv_w_v[0].astype(BF16), row(rwkv_w0[0]), w1, w2, row(rwkv_a0[0]), a1, a2, g1, g2)
    tables = (rwkv_k_k[0], rwkv_k_a[0], rwkv_r_k[0].reshape(-1), rwkv_lnx_w[0], rwkv_lnx_b[0])
    w_o1 = rwkv_w_o[0].astype(BF16)

    def time_mixing(h, shift, x, batch, steps, s0, steps_per_block):
        r, w, k, v, a, g = _time_mix(h, shift, *tm_w)
        tabs = [_pair_table(p.astype(F32), batch) for p in tables]
        y, s_t = _wkv(r, w, k, v, a, _state_to_lanes(s0, batch), *tabs, batch, steps, steps_per_block)
        return _rwkv_out(y, g, x, w_o1), _state_from_lanes(s_t, batch)

    s0_p = jnp.zeros((nb, heads, RWKV_HEAD, RWKV_HEAD), F32)
    xp, wkv_p = time_mixing(hp, shift_p, xp, nb, t_real, s0_p, WKV_STEPS)
    xs, wkv_s = time_mixing(hs, shift_s, xs, db, ds, state_wkv[0].astype(F32), ds)

    ffn1 = (row(ln_ffn[1]),) + ffn_weights(1) + (row(ln_final),)
    _, yp = _ffn(xp, *ffn1)
    _, ys = _ffn(xs, *ffn1)

    seq3 = lambda u: u.reshape(nb, t_pad, -1)[:, :t_real]
    return (yp.reshape(t_pad, nb, d)[N_META:t_real].transpose(1, 0, 2),
            ys.reshape(ds, db, d).transpose(1, 0, 2),
            seq3(c_p)[None],
            seq3(kr_p)[None],
            hp[(t_real - 1) * nb:t_real * nb][None],
            wkv_p.astype(state_wkv.dtype)[None],
            c_s.reshape(db, ds, KV_LORA)[None],
            kr_s.reshape(db, ds, ROPE_DIM)[None],
            hs[(ds - 1) * db:][None],
            wkv_s.astype(state_wkv.dtype)[None])
```

```python
import functools

import jax
import jax.numpy as jnp
from jax import lax
from jax.experimental import pallas as pl
from jax.experimental.pallas import tpu as pltpu

F32 = jnp.float32
BF16 = jnp.bfloat16

N_META = 16
RMS_EPS = 1e-6
MLA_HEADS = 8
Q_LORA = 512
KV_LORA = 256
NOPE_DIM = 128
ROPE_DIM = 64
V_DIM = 128
ROPE_THETA = 10000.0
SOFTMAX_SCALE = (NOPE_DIM + ROPE_DIM) ** -0.5
PAGE_SIZE = 128
RWKV_HEAD = 64
LNX_EPS = 64e-5
MASK_VALUE = -1e30

LANES = 128
QK_DIM = KV_LORA + LANES
ROW_TILE = 512
SEQ_ALIGN = 128
ATTN_Q_TOKENS = 128
ATTN_ROW_CHUNK = 512
ATTN_K_GROUP = 8
DECODE_PAGES = 32
DECODE_GROUPS = 2
FFN_CHUNK = 256
WKV_LANES = 128
WKV_STEPS = 16
VMEM_LIMIT = 56 * 1024 * 1024


def _params(*sem):
    return pltpu.CompilerParams(dimension_semantics=sem, vmem_limit_bytes=VMEM_LIMIT)


def _resident(shape):
    nd = len(shape)
    return pl.BlockSpec(shape, lambda *_: (0,) * nd, pipeline_mode=pl.Buffered(1))


def _rows(width, tile=ROW_TILE):
    return pl.BlockSpec((tile, width), lambda i: (i, 0))


def _rms(x, g):
    return x * lax.rsqrt(jnp.mean(x * x, axis=-1, keepdims=True) + RMS_EPS) * g


def _sigmoid(x):
    return 1.0 / (1.0 + jnp.exp(-x))


def _dot(a, b):
    return jnp.dot(a, b, preferred_element_type=F32)


def _dot_nt(a, b):
    return lax.dot_general(a, b, (((1,), (1,)), ((), ())), preferred_element_type=F32)


def _mla_in_kernel(x_ref, cos_ref, sin_ref, g_ref, win_ref, qn_ref, kvn_ref, wqb_ref, wuk_ref,
                   q_ref, kv_ref, c_ref, kr_ref):
    h = _rms(x_ref[...], g_ref[...]).astype(BF16)
    z = _dot(h, win_ref[...])
    cos = cos_ref[...]
    sin = sin_ref[...]
    c = _rms(z[:, Q_LORA:Q_LORA + KV_LORA], kvn_ref[...])
    kr_lo = Q_LORA + KV_LORA
    kr = z[:, kr_lo:kr_lo + LANES] * cos + z[:, kr_lo + LANES:kr_lo + 2 * LANES] * sin
    c_ref[...] = c
    kr_ref[...] = kr[:, :ROPE_DIM]
    kv_ref[:, :KV_LORA] = c.astype(BF16)
    kv_ref[:, KV_LORA:] = kr.astype(BF16)
    qn = _rms(z[:, :Q_LORA], qn_ref[...]).astype(BF16)
    q = _dot(qn, wqb_ref[...])
    rope_lo = MLA_HEADS * NOPE_DIM
    swap_lo = rope_lo + MLA_HEADS * LANES
    for hh in range(MLA_HEADS):
        ql = _dot(q[:, hh * NOPE_DIM:(hh + 1) * NOPE_DIM].astype(BF16), wuk_ref[hh])
        q_ref[:, hh * QK_DIM:hh * QK_DIM + KV_LORA] = (ql * SOFTMAX_SCALE).astype(BF16)
        qr = (q[:, rope_lo + hh * LANES:rope_lo + (hh + 1) * LANES] * cos
              + q[:, swap_lo + hh * LANES:swap_lo + (hh + 1) * LANES] * sin)
        q_ref[:, hh * QK_DIM + KV_LORA:(hh + 1) * QK_DIM] = (qr * SOFTMAX_SCALE).astype(BF16)


def _mla_in(x, cos, sin, g, w_in, q_norm, kv_norm, w_qb, w_uk):
    m = x.shape[0]
    d = x.shape[1]
    return pl.pallas_call(
        _mla_in_kernel,
        grid=(m // ROW_TILE,),
        in_specs=[_rows(d), _rows(LANES), _rows(LANES), _resident(g.shape), _resident(w_in.shape),
                  _resident(q_norm.shape), _resident(kv_norm.shape), _resident(w_qb.shape),
                  _resident(w_uk.shape)],
        out_specs=[_rows(MLA_HEADS * QK_DIM), _rows(QK_DIM), _rows(KV_LORA), _rows(ROPE_DIM)],
        out_shape=[jax.ShapeDtypeStruct((m, MLA_HEADS * QK_DIM), BF16),
                   jax.ShapeDtypeStruct((m, QK_DIM), BF16),
                   jax.ShapeDtypeStruct((m, KV_LORA), F32),
                   jax.ShapeDtypeStruct((m, ROPE_DIM), F32)],
        compiler_params=_params("parallel"),
        name="mla_in",
    )(x, cos, sin, g, w_in, q_norm, kv_norm, w_qb, w_uk)


def _attn_prompt_kernel(q_ref, kt_ref, v_ref, o_ref, q_scr, m_scr, l_scr, acc_scr):
    qi = pl.program_id(1)
    tq = ATTN_Q_TOKENS
    rows = tq * MLA_HEADS
    for hh in range(MLA_HEADS):
        q_scr[hh * tq:(hh + 1) * tq, :] = q_ref[:, hh * QK_DIM:(hh + 1) * QK_DIM]
    m_scr[...] = jnp.full(m_scr.shape, MASK_VALUE, F32)
    l_scr[...] = jnp.zeros(l_scr.shape, F32)
    acc_scr[...] = jnp.zeros(acc_scr.shape, F32)

    def process(first_block, n_blocks, diagonal):
        kt = [kt_ref[first_block + n] for n in range(n_blocks)]
        kt = kt[0] if n_blocks == 1 else jnp.concatenate(kt, axis=-1)
        start = pl.multiple_of(first_block * tq, tq)
        v = v_ref[pl.ds(start, n_blocks * tq), :KV_LORA]
        for r0 in range(0, rows, ATTN_ROW_CHUNK):
            sl = slice(r0, r0 + ATTN_ROW_CHUNK)
            s = _dot(q_scr[sl, :], kt)
            if diagonal:
                t_loc = (r0 + lax.broadcasted_iota(jnp.int32, (ATTN_ROW_CHUNK, 1), 0)) % tq
                col = lax.broadcasted_iota(jnp.int32, (1, n_blocks * tq), 1) - (n_blocks - 1) * tq
                s = jnp.where(col <= t_loc, s, MASK_VALUE)
            tiles = [s[:, n * tq:(n + 1) * tq] for n in range(n_blocks)]
            tile_max = functools.reduce(jnp.maximum, tiles)
            m_prev = m_scr[sl, :]
            m_new = jnp.maximum(m_prev, jnp.max(tile_max, axis=-1, keepdims=True))
            alpha = jnp.exp(m_prev - m_new)
            ps = [jnp.exp(tile - m_new) for tile in tiles]
            l_scr[sl, :] = alpha * l_scr[sl, :] + functools.reduce(jnp.add, ps)
            p = ps[0] if n_blocks == 1 else jnp.concatenate(ps, axis=-1)
            alpha_wide = jnp.concatenate([alpha] * (KV_LORA // tq), axis=-1)
            acc_scr[sl, :] = alpha_wide * acc_scr[sl, :] + _dot(p.astype(BF16), v)
            m_scr[sl, :] = m_new

    def quad(kj, carry):
        process(ATTN_K_GROUP * kj, ATTN_K_GROUP, False)
        return carry

    lax.fori_loop(0, qi // ATTN_K_GROUP, quad, 0)
    rest = qi % ATTN_K_GROUP
    for n in range(ATTN_K_GROUP):
        pl.when(rest == n)(functools.partial(process, qi - n, n + 1, True))
    for hh in range(MLA_HEADS):
        sl = slice(hh * tq, (hh + 1) * tq)
        l_row = jnp.sum(l_scr[sl, :], axis=-1, keepdims=True)
        o_ref[:, hh * KV_LORA:(hh + 1) * KV_LORA] = (acc_scr[sl, :] / l_row).astype(BF16)


def _attn_prompt(q, kt, kv, n_seq):
    blocks = kt.shape[1]
    t = kv.shape[1]
    tq = ATTN_Q_TOKENS
    rows = tq * MLA_HEADS
    return pl.pallas_call(
        _attn_prompt_kernel,
        grid=(n_seq, blocks),
        in_specs=[pl.BlockSpec((tq, MLA_HEADS * QK_DIM), lambda i, j: (i * blocks + j, 0)),
                  pl.BlockSpec((None, blocks, QK_DIM, tq), lambda i, j: (i, 0, 0, 0)),
                  pl.BlockSpec((None, t, QK_DIM), lambda i, j: (i, 0, 0))],
        out_specs=pl.BlockSpec((tq, MLA_HEADS * KV_LORA), lambda i, j: (i * blocks + j, 0)),
        out_shape=jax.ShapeDtypeStruct((q.shape[0], MLA_HEADS * KV_LORA), BF16),
        scratch_shapes=[pltpu.VMEM((rows, QK_DIM), BF16), pltpu.VMEM((rows, tq), F32),
                        pltpu.VMEM((rows, tq), F32), pltpu.VMEM((rows, KV_LORA), F32)],
        compiler_params=_params("parallel", "arbitrary"),
        name="attn_prompt",
    )(q, kt, kv)


def _attn_decode_kernel(pt_ref, q_ref, cn_ref, krn_ref, lat_hbm, rope_hbm, o_ref,
                        lat_buf, rope_buf, sem, *, n_new, n_groups):
    b = pl.program_id(0)
    q = q_ref[...]
    ql = q[:, :KV_LORA]
    qr = q[:, KV_LORA:KV_LORA + ROPE_DIM]
    rows = q.shape[0]

    def page_copies(page, slot, i):
        return (pltpu.make_async_copy(lat_hbm.at[0, page], lat_buf.at[slot, i], sem.at[slot, 0]),
                pltpu.make_async_copy(rope_hbm.at[0, page], rope_buf.at[slot, i], sem.at[slot, 1]))

    def start_group(row, group, slot):
        for i in range(DECODE_PAGES):
            for cp in page_copies(pt_ref[row, group * DECODE_PAGES + i], slot, i):
                cp.start()

    def wait_group(slot):
        for i in range(DECODE_PAGES):
            for cp in page_copies(0, slot, i):
                cp.wait()

    @pl.when(b == 0)
    def _():
        start_group(0, 0, 0)

    qlf = ql.astype(F32)
    qrf = qr.astype(F32)
    cn = cn_ref[...].astype(BF16).astype(F32)
    krn = krn_ref[...].astype(BF16).astype(F32)
    q_tok = lax.broadcasted_iota(jnp.int32, (rows, 1), 0) % n_new
    cols = []
    for t in range(n_new):
        s_t = (jnp.sum(qlf * cn[t:t + 1, :], axis=-1, keepdims=True)
               + jnp.sum(qrf * krn[t:t + 1, :], axis=-1, keepdims=True))
        cols.append(jnp.where(q_tok >= t, s_t, MASK_VALUE))
    m_run = functools.reduce(jnp.maximum, cols)
    l_run = jnp.zeros((rows, 1), F32)
    acc = jnp.zeros((rows, KV_LORA), F32)
    for t in range(n_new):
        p_t = jnp.exp(cols[t] - m_run)
        l_run = l_run + p_t
        acc = acc + p_t.astype(BF16).astype(F32) * cn[t:t + 1, :]

    per_part = DECODE_PAGES // DECODE_GROUPS
    for group in range(n_groups):
        slot = group % 2
        if group + 1 < n_groups:
            start_group(b, group + 1, 1 - slot)
        else:
            pl.when(b + 1 < pl.num_programs(0))(functools.partial(start_group, b + 1, 0, 1 - slot))
        wait_group(slot)
        partials = []
        for part in range(DECODE_GROUPS):
            idx = range(part * per_part, (part + 1) * per_part)
            kbs = [lat_buf[slot, i].astype(BF16) for i in idx]
            s = jnp.concatenate(
                [_dot_nt(ql, kb) + _dot(qr, rope_buf[slot, i].astype(BF16)) for i, kb in zip(idx, kbs)],
                axis=-1)
            m_g = jnp.max(s, axis=-1, keepdims=True)
            p = jnp.exp(s - m_g)
            l_g = jnp.sum(p, axis=-1, keepdims=True)
            pb = p.astype(BF16)
            acc_g = _dot(pb[:, :PAGE_SIZE], kbs[0])
            for n in range(1, per_part):
                acc_g = acc_g + _dot(pb[:, n * PAGE_SIZE:(n + 1) * PAGE_SIZE], kbs[n])
            partials.append((m_g, l_g, acc_g))
        m_new = functools.reduce(jnp.maximum, [m_g for m_g, _, _ in partials], m_run)
        alpha = jnp.exp(m_run - m_new)
        l_run = alpha * l_run
        acc = alpha * acc
        for m_g, l_g, acc_g in partials:
            w_g = jnp.exp(m_g - m_new)
            l_run = l_run + w_g * l_g
            acc = acc + w_g * acc_g
        m_run = m_new
    o_ref[...] = (acc / l_run).astype(BF16)


def _attn_decode(page_table, q, c_new, kr_new, cache_latent, cache_krope_t):
    db, rows, _ = q.shape
    n_new = c_new.shape[1]
    n_groups = page_table.shape[1] // DECODE_PAGES
    assert n_groups % 2 == 0, "the prefetch of the next row's first group assumes it lands in slot 0"
    grid_spec = pltpu.PrefetchScalarGridSpec(
        num_scalar_prefetch=1,
        grid=(db,),
        in_specs=[pl.BlockSpec((None, rows, QK_DIM), lambda b, pt: (b, 0, 0)),
                  pl.BlockSpec((None, n_new, KV_LORA), lambda b, pt: (b, 0, 0)),
                  pl.BlockSpec((None, n_new, ROPE_DIM), lambda b, pt: (b, 0, 0)),
                  pl.BlockSpec(memory_space=pl.ANY),
                  pl.BlockSpec(memory_space=pl.ANY)],
        out_specs=pl.BlockSpec((None, rows, KV_LORA), lambda b, pt: (b, 0, 0)),
        scratch_shapes=[pltpu.VMEM((2, DECODE_PAGES, PAGE_SIZE, KV_LORA), F32),
                        pltpu.VMEM((2, DECODE_PAGES, ROPE_DIM, PAGE_SIZE), F32),
                        pltpu.SemaphoreType.DMA((2, 2))],
    )
    return pl.pallas_call(
        functools.partial(_attn_decode_kernel, n_new=n_new, n_groups=n_groups),
        grid_spec=grid_spec,
        out_shape=jax.ShapeDtypeStruct((db, rows, KV_LORA), BF16),
        compiler_params=_params("arbitrary"),
        name="attn_decode",
    )(page_table, q, c_new, kr_new, cache_latent, cache_krope_t)


def _mla_out_kernel(o_ref, x_ref, wuv_ref, wo_ref, out_ref):
    us = [_dot(o_ref[:, hh * KV_LORA:(hh + 1) * KV_LORA], wuv_ref[hh]).astype(BF16)
          for hh in range(MLA_HEADS)]
    out_ref[...] = x_ref[...] + _dot(jnp.concatenate(us, axis=-1), wo_ref[...])


def _mla_out(o, x, w_uv, w_o):
    m, d = x.shape
    return pl.pallas_call(
        _mla_out_kernel,
        grid=(m // ROW_TILE,),
        in_specs=[_rows(o.shape[1]), _rows(d), _resident(w_uv.shape), _resident(w_o.shape)],
        out_specs=_rows(d),
        out_shape=jax.ShapeDtypeStruct((m, d), F32),
        compiler_params=_params("parallel"),
        name="mla_out",
    )(o, x, w_uv, w_o)


def _ffn_kernel(x_ref, g_ref, wgu_ref, wd_ref, gn_ref, x_out_ref, hn_ref):
    x = x_ref[...]
    h = _rms(x, g_ref[...]).astype(BF16)
    acc = x
    for ci in range(wgu_ref.shape[0]):
        gu = _dot(h, wgu_ref[ci])
        gate = gu[:, :FFN_CHUNK]
        act = (gate * _sigmoid(gate) * gu[:, FFN_CHUNK:]).astype(BF16)
        acc = acc + _dot(act, wd_ref[ci])
    x_out_ref[...] = acc
    hn_ref[...] = _rms(acc, gn_ref[...])


def _ffn(x, g, wgu, wd, g_next):
    m, d = x.shape
    return pl.pallas_call(
        _ffn_kernel,
        grid=(m // ROW_TILE,),
        in_specs=[_rows(d), _resident(g.shape), _resident(wgu.shape), _resident(wd.shape),
                  _resident(g_next.shape)],
        out_specs=[_rows(d), _rows(d)],
        out_shape=[jax.ShapeDtypeStruct((m, d), F32), jax.ShapeDtypeStruct((m, d), F32)],
        compiler_params=_params("parallel"),
        name="ffn",
    )(x, g, wgu, wd, g_next)


def _time_mix_kernel(h_ref, tail_ref, first_ref, mix_ref, wr_ref, wk_ref, wv_ref, w0_ref, w1_ref, w2_ref,
                     a0_ref, a1_ref, a2_ref, g1_ref, g2_ref,
                     r_ref, w_ref, k_ref, v_ref, a_ref, g_ref):
    h = h_ref[...]
    batch = tail_ref.shape[0]
    before = jnp.where(pl.program_id(0) == 0, first_ref[...], tail_ref[...])
    xx = jnp.concatenate([before, h[:-batch]], axis=0) - h
    mix = mix_ref[...]

    def mixed(i):
        return (h + xx * mix[i:i + 1, :]).astype(BF16)

    def store_pairs(ref, val):
        for p in range(ref.shape[0]):
            ref[p] = val[:, p * LANES:(p + 1) * LANES]

    store_pairs(r_ref, _dot(mixed(0), wr_ref[...]))
    lw = jnp.tanh(_dot(mixed(1), w1_ref[...])).astype(BF16)
    z = -(w0_ref[...] + _dot(lw, w2_ref[...]))
    softplus = jnp.maximum(z, 0.0) + jnp.log(1.0 + jnp.exp(-jnp.abs(z)))
    store_pairs(w_ref, jnp.exp(-jnp.exp(-softplus - 0.5)))
    store_pairs(k_ref, _dot(mixed(2), wk_ref[...]))
    store_pairs(v_ref, _dot(mixed(3), wv_ref[...]))
    al = _dot(mixed(4), a1_ref[...]).astype(BF16)
    store_pairs(a_ref, _sigmoid(a0_ref[...] + _dot(al, a2_ref[...])))
    gl = _sigmoid(_dot(mixed(5), g1_ref[...])).astype(BF16)
    g_ref[...] = _dot(gl, g2_ref[...])


def _pair_rows(pairs, tile=ROW_TILE):
    return pl.BlockSpec((pairs, tile, LANES), lambda i: (0, i, 0))


def _time_mix(h, first, mix, w_r, w_k, w_v, w0, w1, w2, a0, a1, a2, g1, g2):
    m, d = h.shape
    batch = first.shape[0]
    pairs = d // LANES
    tiles_per_batch = ROW_TILE // batch
    weights = (mix, w_r, w_k, w_v, w0, w1, w2, a0, a1, a2, g1, g2)
    tail = pl.BlockSpec((batch, d), lambda i: (jnp.maximum(i * tiles_per_batch - 1, 0), 0))
    return pl.pallas_call(
        _time_mix_kernel,
        grid=(m // ROW_TILE,),
        in_specs=[_rows(d), tail, _resident(first.shape)] + [_resident(w.shape) for w in weights],
        out_specs=[_pair_rows(pairs)] * 5 + [_rows(d)],
        out_shape=[jax.ShapeDtypeStruct((pairs, m, LANES), F32)] * 5 + [jax.ShapeDtypeStruct((m, d), F32)],
        compiler_params=_params("parallel"),
        name="time_mix",
    )(h, h, first, *weights)


def _wkv_kernel(r_ref, w_ref, k_ref, v_ref, a_ref, s0_ref, kk_ref, ka_ref, rk_ref, lnw_ref, lnb_ref,
                y_ref, st_ref, s_scr, v_scr, y_scr, p_scr, c_scr, *, rows_per_pair, steps, real_blocks):
    step_block = pl.program_id(1)
    n = s_scr.shape[1]
    pairs = r_ref.shape[0]

    @pl.when(step_block == 0)
    def _():
        s_scr[...] = s0_ref[...]

    def channels_by_lane(ref, off):
        tile = [ref[p, pl.ds(off, rows_per_pair), :] for p in range(pairs)]
        tile = tile[0] if pairs == 1 else jnp.concatenate(tile, axis=0)
        return tile.T

    def step(t, carry):
        off = pl.multiple_of(t * rows_per_pair, rows_per_pair)
        r2 = channels_by_lane(r_ref, off)
        w2 = channels_by_lane(w_ref, off)
        k2 = channels_by_lane(k_ref, off)
        a2 = channels_by_lane(a_ref, off)
        v_scr[...] = channels_by_lane(v_ref, off)
        for half in range(2):
            sl = slice(half * n, (half + 1) * n)
            r, w, k, a = r2[sl], w2[sl], k2[sl], a2[sl]
            kk = k * kk_ref[half]
            kk = kk / jnp.maximum(jnp.sqrt(jnp.sum(kk * kk, axis=0, keepdims=True)), 1e-12)
            b = kk * a
            k = k * (1.0 + (a - 1.0) * ka_ref[half])
            prod_prev = p_scr[half]
            prod = prod_prev * w
            p_scr[half] = prod
            inv = 1.0 / prod
            c_scr[0] = kk * prod_prev
            c_scr[1] = b * inv
            c_scr[2] = k * inv
            c_scr[3] = r * prod
            v = v_scr[sl, :]
            sa = jnp.zeros_like(v)
            for j in range(n):
                sa = sa - s_scr[half, j] * c_scr[0, j:j + 1, :]
            y = jnp.zeros_like(v)
            for j in range(n):
                s_j = s_scr[half, j] + sa * c_scr[1, j:j + 1, :] + v * c_scr[2, j:j + 1, :]
                s_scr[half, j] = s_j
                y = y + s_j * c_scr[3, j:j + 1, :]
            mu = jnp.mean(y, axis=0, keepdims=True)
            yc = y - mu
            var = jnp.mean(yc * yc, axis=0, keepdims=True)
            bonus = jnp.sum(r * k * rk_ref[half], axis=0, keepdims=True) * v_scr[sl, :]
            y_scr[sl, :] = yc * lax.rsqrt(var + LNX_EPS) * lnw_ref[half] + lnb_ref[half] + bonus
        y_rows = y_scr[...].T
        for p in range(pairs):
            y_ref[p, pl.ds(off, rows_per_pair), :] = y_rows[p * rows_per_pair:(p + 1) * rows_per_pair, :]
        return carry

    @pl.when(step_block < real_blocks)
    def _():
        p_scr[...] = jnp.ones(p_scr.shape, F32)
        lax.fori_loop(0, steps, step, 0, unroll=4)
        for half in range(2):

            def rescale(j, c2, half=half):
                s_scr[half, j] = s_scr[half, j] * p_scr[half, pl.ds(j, 1), :]
                return c2

            lax.fori_loop(0, n, rescale, 0, unroll=8)

    @pl.when(step_block >= real_blocks)
    def _():
        y_ref[...] = jnp.zeros(y_ref.shape, F32)

    @pl.when(step_block == real_blocks - 1)
    def _():
        st_ref[...] = s_scr[...]


def _wkv(r, w, k, v, a, s0, k_k, k_a, r_k, lnw, lnb, batch, real_steps, steps_per_block):
    pairs, rows, _ = r.shape
    n = RWKV_HEAD
    group = WKV_LANES // batch
    block_rows = steps_per_block * batch
    kern = functools.partial(_wkv_kernel, rows_per_pair=batch, steps=steps_per_block,
                             real_blocks=real_steps // steps_per_block)
    seq = pl.BlockSpec((group, block_rows, LANES), lambda i, j: (i, j, 0))
    state = pl.BlockSpec((None, 2, n, n, WKV_LANES), lambda i, j: (i, 0, 0, 0, 0))
    table = pl.BlockSpec((None, 2, n, WKV_LANES), lambda i, j: (i, 0, 0, 0))
    return pl.pallas_call(
        kern,
        grid=(pairs // group, rows // block_rows),
        in_specs=[seq] * 5 + [state] + [table] * 5,
        out_specs=[seq, state],
        out_shape=[jax.ShapeDtypeStruct(r.shape, F32), jax.ShapeDtypeStruct(s0.shape, F32)],
        scratch_shapes=[pltpu.VMEM((2, n, n, WKV_LANES), F32), pltpu.VMEM((2 * n, WKV_LANES), F32),
                        pltpu.VMEM((2 * n, WKV_LANES), F32), pltpu.VMEM((2, n, WKV_LANES), F32),
                        pltpu.VMEM((4, n, WKV_LANES), F32)],
        compiler_params=_params("parallel", "arbitrary"),
        name="wkv",
    )(r, w, k, v, a, s0, k_k, k_a, r_k, lnw, lnb)


def _rwkv_out_kernel(y_ref, g_ref, x_ref, wo_ref, out_ref):
    y = jnp.concatenate([y_ref[p] for p in range(y_ref.shape[0])], axis=-1)
    out_ref[...] = x_ref[...] + _dot((y * g_ref[...]).astype(BF16), wo_ref[...])


def _rwkv_out(y, g, x, w_o):
    m, d = x.shape
    return pl.pallas_call(
        _rwkv_out_kernel,
        grid=(m // ROW_TILE,),
        in_specs=[_pair_rows(y.shape[0]), _rows(d), _rows(d), _resident(w_o.shape)],
        out_specs=_rows(d),
        out_shape=jax.ShapeDtypeStruct((m, d), F32),
        compiler_params=_params("parallel"),
        name="rwkv_out",
    )(y, g, x, w_o)


def _rope_tables(pos, reps):
    half = ROPE_DIM // 2
    inv = ROPE_THETA ** (-jnp.arange(half, dtype=F32) / half)
    ang = pos.astype(F32)[:, None] * inv[None, :]
    cos, sin = jnp.cos(ang), jnp.sin(ang)
    cos = jnp.tile(jnp.concatenate([cos, cos], axis=-1), (reps, LANES // ROPE_DIM))
    sin = jnp.tile(jnp.concatenate([-sin, sin], axis=-1), (reps, LANES // ROPE_DIM))
    return cos, sin


def _swap_halves(w):
    half = w.shape[-1] // 2
    return jnp.concatenate([w[..., half:], w[..., :half]], axis=-1)


def _pad_lanes(w):
    return jnp.pad(w, [(0, 0)] * (w.ndim - 1) + [(0, LANES - w.shape[-1])])


def _time_major(x, batch):
    d = x.shape[-1]
    return x.reshape(batch, -1, d).transpose(1, 0, 2).reshape(-1, d)


def _pair_table(p, batch):
    pairs = p.shape[0] // LANES
    group = WKV_LANES // batch
    t = p.reshape(pairs // group, group, 2, RWKV_HEAD).transpose(0, 2, 3, 1)
    return jnp.repeat(t, batch, axis=-1)


def _state_to_lanes(s, batch):
    heads, n = s.shape[1], s.shape[2]
    group = WKV_LANES // batch
    s = s.reshape(batch, heads // (2 * group), group, 2, n, n)
    return s.transpose(1, 3, 5, 4, 2, 0).reshape(heads // (2 * group), 2, n, n, WKV_LANES)


def _state_from_lanes(s, batch):
    groups, _, n, _, _ = s.shape
    group = WKV_LANES // batch
    s = s.reshape(groups, 2, n, n, group, batch).transpose(5, 0, 4, 1, 3, 2)
    return s.reshape(batch, groups * group * 2, n, n)


def kernel(x_prompt, x_sample, cache_latent, cache_krope, state_shift, state_wkv, page_table, meta_tokens, ln_mix, ln_ffn, ln_final, attn_w_in, attn_q_norm, attn_kv_norm, attn_w_qb, attn_w_uk, attn_w_uv, attn_w_o, rwkv_mix, rwkv_w_r, rwkv_w_k, rwkv_w_v, rwkv_w_o, rwkv_w0, rwkv_w1, rwkv_w2, rwkv_a0, rwkv_a1, rwkv_a2, rwkv_g1, rwkv_g2, rwkv_k_k, rwkv_k_a, rwkv_r_k, rwkv_lnx_w, rwkv_lnx_b, ffn_w_gate, ffn_w_up, ffn_w_down):
    nb, seq, d = x_prompt.shape
    db, ds, _ = x_sample.shape
    t_real = seq + N_META
    t_pad = -(-t_real // SEQ_ALIGN) * SEQ_ALIGN
    past_len = page_table.shape[1] * PAGE_SIZE
    heads = d // RWKV_HEAD
    row = lambda p: p.reshape(1, -1).astype(F32)

    meta = jnp.broadcast_to(meta_tokens[None].astype(x_prompt.dtype), (nb, N_META, d))
    xp = jnp.concatenate([meta, x_prompt, jnp.zeros((nb, t_pad - t_real, d), x_prompt.dtype)], axis=1)
    xp = xp.reshape(nb * t_pad, d)
    xs = x_sample.reshape(db * ds, d)
    cos_p, sin_p = _rope_tables(jnp.arange(t_pad), nb)
    cos_s, sin_s = _rope_tables(past_len + jnp.arange(ds), db)

    w_in = attn_w_in[0]
    kr_cols = w_in[:, Q_LORA + KV_LORA:]
    w_in_ext = jnp.concatenate(
        [w_in[:, :Q_LORA + KV_LORA], _pad_lanes(kr_cols), _pad_lanes(_swap_halves(kr_cols))],
        axis=-1).astype(BF16)
    w_qb = attn_w_qb[0].reshape(Q_LORA, MLA_HEADS, NOPE_DIM + ROPE_DIM)
    q_rope_cols = w_qb[:, :, NOPE_DIM:]
    w_qb_ext = jnp.concatenate(
        [w_qb[:, :, :NOPE_DIM].reshape(Q_LORA, -1), _pad_lanes(q_rope_cols).reshape(Q_LORA, -1),
         _pad_lanes(_swap_halves(q_rope_cols)).reshape(Q_LORA, -1)], axis=-1).astype(BF16)
    w_uk_t = attn_w_uk[0].transpose(1, 2, 0).astype(BF16)
    w_uv = attn_w_uv[0].transpose(1, 0, 2).astype(BF16)
    w_o = attn_w_o[0].astype(BF16)
    mla_w = (row(ln_mix[0]), w_in_ext, row(attn_q_norm[0]), row(attn_kv_norm[0]), w_qb_ext, w_uk_t)

    q_p, kv_p, c_p, kr_p = _mla_in(xp, cos_p, sin_p, *mla_w)
    q_s, _, c_s, kr_s = _mla_in(xs, cos_s, sin_s, *mla_w)

    kv3 = kv_p.reshape(nb, t_pad, QK_DIM)
    kt4 = kv3.reshape(nb, t_pad // ATTN_Q_TOKENS, ATTN_Q_TOKENS, QK_DIM).transpose(0, 1, 3, 2)
    o_p = _attn_prompt(q_p, kt4, kv3, nb)
    q_s3 = q_s.reshape(db, ds, MLA_HEADS, QK_DIM).transpose(0, 2, 1, 3).reshape(db, MLA_HEADS * ds, QK_DIM)
    o_s = _attn_decode(page_table, q_s3, c_s.reshape(db, ds, KV_LORA), kr_s.reshape(db, ds, ROPE_DIM),
                       cache_latent[:1], jnp.swapaxes(cache_krope[:1], 2, 3))
    o_s = o_s.reshape(db, MLA_HEADS, ds, KV_LORA).transpose(0, 2, 1, 3).reshape(db * ds, MLA_HEADS * KV_LORA)

    def ffn_weights(i):
        n_chunks = ffn_w_gate.shape[2] // FFN_CHUNK
        wg = ffn_w_gate[i].reshape(d, n_chunks, FFN_CHUNK)
        wu = ffn_w_up[i].reshape(d, n_chunks, FFN_CHUNK)
        wgu = jnp.concatenate([wg, wu], axis=-1).transpose(1, 0, 2).astype(BF16)
        wd = ffn_w_down[i].reshape(n_chunks, FFN_CHUNK, d).astype(BF16)
        return wgu, wd

    ffn0 = (row(ln_ffn[0]),) + ffn_weights(0) + (row(ln_mix[1]),)
    xp, hp = _ffn(_mla_out(o_p, xp, w_uv, w_o), *ffn0)
    xs, hs = _ffn(_mla_out(o_s, xs, w_uv, w_o), *ffn0)

    xp, hp, xs, hs = _time_major(xp, nb), _time_major(hp, nb), _time_major(xs, db), _time_major(hs, db)
    shift_p = jnp.zeros((nb, d), F32)
    shift_s = state_shift[0].astype(F32)

    def lora(w_down, w_up):
        rank = w_down.shape[1]
        pad = -(-rank // LANES) * LANES - rank
        return (jnp.pad(w_down, ((0, 0), (0, pad))).astype(BF16),
                jnp.pad(w_up, ((0, pad), (0, 0))).astype(BF16))

    w1, w2 = lora(rwkv_w1[0], rwkv_w2[0])
    a1, a2 = lora(rwkv_a1[0], rwkv_a2[0])
    g1, g2 = lora(rwkv_g1[0], rwkv_g2[0])
    tm_w = (rwkv_mix[0].astype(F32), rwkv_w_r[0].astype(BF16), rwkv_w_k[0].astype(BF16),
            rwkv_w_v[0].astype(BF16), row(rwkv_w0[0]), w1, w2, row(rwkv_a0[0]), a1, a2, g1, g2)
    tables = (rwkv_k_k[0], rwkv_k_a[0], rwkv_r_k[0].reshape(-1), rwkv_lnx_w[0], rwkv_lnx_b[0])
    w_o1 = rwkv_w_o[0].astype(BF16)

    def time_mixing(h, shift, x, batch, steps, s0, steps_per_block):
        r, w, k, v, a, g = _time_mix(h, shift, *tm_w)
        tabs = [_pair_table(p.astype(F32), batch) for p in tables]
        y, s_t = _wkv(r, w, k, v, a, _state_to_lanes(s0, batch), *tabs, batch, steps, steps_per_block)
        return _rwkv_out(y, g, x, w_o1), _state_from_lanes(s_t, batch)

    s0_p = jnp.zeros((nb, heads, RWKV_HEAD, RWKV_HEAD), F32)
    xp, wkv_p = time_mixing(hp, shift_p, xp, nb, t_real, s0_p, WKV_STEPS)
    xs, wkv_s = time_mixing(hs, shift_s, xs, db, ds, state_wkv[0].astype(F32), ds)

    ffn1 = (row(ln_ffn[1]),) + ffn_weights(1) + (row(ln_final),)
    _, yp = _ffn(xp, *ffn1)
    _, ys = _ffn(xs, *ffn1)

    seq3 = lambda u: u.reshape(nb, t_pad, -1)[:, :t_real]
    return (yp.reshape(t_pad, nb, d)[N_META:t_real].transpose(1, 0, 2),
            ys.reshape(ds, db, d).transpose(1, 0, 2),
            seq3(c_p)[None],
            seq3(kr_p)[None],
            hp[(t_real - 1) * nb:t_real * nb][None],
            wkv_p.astype(state_wkv.dtype)[None],
            c_s.reshape(db, ds, KV_LORA)[None],
            kr_s.reshape(db, ds, ROPE_DIM)[None],
            hs[(ds - 1) * db:][None],
            wkv_s.astype(state_wkv.dtype)[None])
```
